```python
import jax, jax.numpy as jnp
from jax import lax
import numpy as np

D_MODEL = 1024
BATCH = 16
SEQ = 256
DEPTH = 2
DEC_BATCH = 4
DEC_SEQ = 4096
PAST_LEN = 256

GRID_W = 64
N_MIXERS = 2
N_MLSTM = (DEPTH + 1) // 2
N_CONV = DEPTH // 2
M_INNER = 2 * D_MODEL
M_HEADS = 4
M_HEAD_DIM = M_INNER // M_HEADS
M_CONV_W = 5
CHUNK = 128
CONF_CONV_W = 31
FFN_DIM = 2816
N_SUB = 3
N_MOD = 3 * N_SUB
ALPHA = (2 * DEPTH) ** 0.25
BETA = (8 * DEPTH) ** -0.25
LN_EPS = 1e-5

kernel_name = "hybrid_mlstm_conformer_diffusion_step"


def layer_norm(x, g, b):
    xf = x.astype(jnp.float32)
    mu = jnp.mean(xf, -1, keepdims=True)
    var = jnp.mean(jnp.square(xf - mu), -1, keepdims=True)
    return ((xf - mu) * lax.rsqrt(var + LN_EPS) * g + b).astype(x.dtype)


def modulate(x, mod, s):
    return x * (1 + mod[:, 3 * s + 1][:, None]) + mod[:, 3 * s][:, None]


def post_norm(x, y, mod, s, g, b, res_w):
    return layer_norm(ALPHA * x + res_w * mod[:, 3 * s + 2][:, None] * y, g, b)


def swiglu(x, w_in, w_out):
    a, u = jnp.split(x @ w_in, 2, axis=-1)
    return (jax.nn.silu(a) * u) @ w_out


def dwconv(x, w, b):
    K = w.shape[0]
    y = lax.conv_general_dilated(x, w[:, None, :].astype(x.dtype), (1,), [(K // 2, K // 2)],
                                 dimension_numbers=('NWC', 'WIO', 'NWC'),
                                 feature_group_count=x.shape[-1])
    return y + b


def conformer_conv(x, n_rows, w_pw1, b_pw1, w_dw, b_dw, g_ln, b_ln, w_pw2, b_pw2):
    a, gl = jnp.split(x @ w_pw1 + b_pw1, 2, axis=-1)
    h = a * jax.nn.sigmoid(gl)
    B, T, C = h.shape
    h = dwconv(h.reshape(B * n_rows, T // n_rows, C), w_dw, b_dw).reshape(B, T, C)
    h = jax.nn.silu(layer_norm(h, g_ln, b_ln))
    return h @ w_pw2 + b_pw2


def mlstm_chunkwise(q, k, v, li, lf, C0, n0, m0):
    B, H, T, DK = q.shape
    L = min(CHUNK, T)
    NC = T // L

    def chunks(a):
        return jnp.moveaxis(a.reshape(B, H, NC, L, *a.shape[3:]), 2, 0)

    lower = jnp.tril(jnp.ones((L, L), bool))

    def step(carry, xs):
        C, n, m = carry
        qq, kk, vv, ii, ff = xs
        b = jnp.cumsum(ff, axis=-1)
        dmat = jnp.where(lower, b[..., :, None] - b[..., None, :] + ii[..., None, :], -jnp.inf)
        inter = b + m[..., None]
        mr = jnp.maximum(jnp.max(dmat, -1), inter)
        s = jnp.einsum('bhtk,bhsk->bhts', qq, kk) * jnp.exp(dmat - mr[..., None])
        w_in = jnp.exp(inter - mr)
        num = jnp.einsum('bhts,bhsv->bhtv', s, vv) + w_in[..., None] * jnp.einsum('bhtk,bhkv->bhtv', qq, C)
        den = jnp.sum(s, -1) + w_in * jnp.einsum('bhtk,bhk->bht', qq, n)
        h = num / jnp.maximum(jnp.abs(den), jnp.exp(-mr))[..., None]
        g = b[..., -1]
        lw = g[..., None] - b + ii
        m_new = jnp.maximum(g + m, jnp.max(lw, -1))
        decay = jnp.exp(g + m - m_new)
        kw = kk * jnp.exp(lw - m_new[..., None])[..., None]
        C_new = decay[..., None, None] * C + jnp.einsum('bhsk,bhsv->bhkv', kw, vv)
        n_new = decay[..., None] * n + jnp.sum(kw, axis=2)
        return (C_new, n_new, m_new), h

    (C, n, m), hs = lax.scan(step, (C0, n0, m0), tuple(map(chunks, (q, k, v, li, lf))))
    h = jnp.moveaxis(hs, 0, 2).reshape(B, H, T, -1)
    return h, C, n, m


def head_norm(h, g):
    B, H, T, DV = h.shape
    mu = jnp.mean(h, -1, keepdims=True)
    var = jnp.mean(jnp.square(h - mu), -1, keepdims=True)
    hn = (h - mu) * lax.rsqrt(var + LN_EPS)
    return hn.transpose(0, 2, 1, 3).reshape(B, T, H * DV) * g


def mlstm_mixer(x, w_up, w_conv, b_conv, w_qk, w_v, w_gate, b_gate, w_o, b_o, norm_g, skip, w_down,
                C0, n0, m0):
    B, T, _ = x.shape
    f32 = jnp.float32
    xm, z = jnp.split(x @ w_up, 2, axis=-1)
    xc = jax.nn.silu(dwconv(xm, w_conv, b_conv))
    q, k = jnp.split(xc @ w_qk, 2, axis=-1)
    v = xm @ w_v
    pre = (jnp.concatenate([q, k, v], -1) @ w_gate + b_gate).astype(f32)
    pre = pre.reshape(B, T, 2, 2, M_HEADS).transpose(2, 3, 0, 4, 1)

    def heads(a):
        return a.reshape(B, T, M_HEADS, -1).transpose(0, 2, 1, 3).astype(f32)

    qh = heads(q) * (M_HEAD_DIM ** -0.5)
    kh = heads(k)
    vh = heads(v)
    h_sum = None
    Cs, ns, ms = [], [], []
    for d in range(2):
        args = (qh, kh, vh, pre[d, 0], jax.nn.log_sigmoid(pre[d, 1]))
        if d == 1:
            args = tuple(jnp.flip(a, axis=2) for a in args)
        h, C, n, m = mlstm_chunkwise(*args, C0[:, d].astype(f32), n0[:, d].astype(f32), m0[:, d].astype(f32))
        if d == 1:
            h = jnp.flip(h, axis=2)
        h_sum = h if h_sum is None else h_sum + h
        Cs.append(C); ns.append(n); ms.append(m)
    hn = head_norm(h_sum, norm_g).astype(x.dtype)
    o = jax.nn.sigmoid(xm @ w_o + b_o)
    out = (o * hn + skip * xc) * jax.nn.silu(z)
    return (out @ w_down,
            jnp.stack(Cs, 1).astype(x.dtype), jnp.stack(ns, 1).astype(x.dtype), jnp.stack(ms, 1).astype(x.dtype))


def setup_inputs(seed: int = 0) -> dict:
    key = jax.random.key(seed)
    ks = iter(jax.random.split(key, 48))
    f32 = jnp.float32
    D, DI, H, DH, F = D_MODEL, M_INNER, M_HEADS, M_HEAD_DIM, FFN_DIM

    def nrm(shape, scale):
        return jax.random.normal(next(ks), shape, f32) * scale

    b_gate = jnp.concatenate([nrm((N_MLSTM, 2, 1, H), 0.1),
                              jnp.linspace(3.0, 6.0, H, dtype=f32) + nrm((N_MLSTM, 2, 1, H), 0.1)],
                             axis=2).reshape(N_MLSTM, 4 * H)
    return {
        "x_prompt": nrm((BATCH, SEQ, D), 1.0),
        "x_sample": nrm((DEC_BATCH, DEC_SEQ, D), 1.0),
        "state_C": nrm((DEC_BATCH, N_MLSTM, 2, H, DH, DH), 0.05),
        "state_n": nrm((DEC_BATCH, N_MLSTM, 2, H, DH), 0.5),
        "state_m": nrm((DEC_BATCH, N_MLSTM, 2, H), 0.5),
        "c": nrm((DEC_BATCH, D), 1.0),
        "c_ctx": nrm((D,), 1.0),
        "w_mod": nrm((DEPTH, D, N_MOD * D), D ** -0.5),
        "b_mod": nrm((DEPTH, N_MOD * D), 0.02),
        "ln_g": 1.0 + nrm((DEPTH, N_SUB, D), 0.02),
        "ln_b": nrm((DEPTH, N_SUB, D), 0.02),
        "ffn_w_in": nrm((DEPTH, 2, D, 2 * F), D ** -0.5),
        "ffn_w_out": nrm((DEPTH, 2, F, D), BETA * F ** -0.5),
        "m_w_up": nrm((N_MLSTM, D, 2 * DI), D ** -0.5),
        "m_w_conv": nrm((N_MLSTM, M_CONV_W, DI), M_CONV_W ** -0.5),
        "m_b_conv": nrm((N_MLSTM, DI), 0.02),
        "m_w_qk": nrm((N_MLSTM, DI, 2 * DI), DI ** -0.5),
        "m_w_v": nrm((N_MLSTM, DI, DI), DI ** -0.5),
        "m_w_gate": nrm((N_MLSTM, 3 * DI, 4 * H), 0.1 * (3 * DI) ** -0.5),
        "m_b_gate": b_gate,
        "m_w_o": nrm((N_MLSTM, DI, DI), DI ** -0.5),
        "m_b_o": nrm((N_MLSTM, DI), 0.02),
        "m_norm_g": 1.0 + nrm((N_MLSTM, DI), 0.02),
        "m_skip": 1.0 + nrm((N_MLSTM, DI), 0.02),
        "m_w_down": nrm((N_MLSTM, DI, D), BETA * DI ** -0.5),
        "cv_w_pw1": nrm((N_CONV, D, 2 * D), D ** -0.5),
        "cv_b_pw1": nrm((N_CONV, 2 * D), 0.02),
        "cv_w_dw": nrm((N_CONV, CONF_CONV_W, D), CONF_CONV_W ** -0.5),
        "cv_b_dw": nrm((N_CONV, D), 0.02),
        "cv_ln_g": 1.0 + nrm((N_CONV, D), 0.02),
        "cv_ln_b": nrm((N_CONV, D), 0.02),
        "cv_w_pw2": nrm((N_CONV, D, D), BETA * D ** -0.5),
        "cv_b_pw2": nrm((N_CONV, D), 0.02),
    }


def reference(x_prompt, x_sample, state_C, state_n, state_m, c, c_ctx,
              w_mod, b_mod, ln_g, ln_b, ffn_w_in, ffn_w_out,
              m_w_up, m_w_conv, m_b_conv, m_w_qk, m_w_v, m_w_gate, m_b_gate,
              m_w_o, m_b_o, m_norm_g, m_skip, m_w_down,
              cv_w_pw1, cv_b_pw1, cv_w_dw, cv_b_dw, cv_ln_g, cv_ln_b, cv_w_pw2, cv_b_pw2):
    f32 = jnp.float32
    bp = x_prompt.shape[0]
    rows = x_sample.shape[1] // GRID_W
    cond_p = jax.nn.silu(c_ctx)[None]
    cond_s = jax.nn.silu(c)
    xp, xs = x_prompt, x_sample
    new_C, new_n, new_m = [], [], []
    for i in range(DEPTH):
        mp = (cond_p @ w_mod[i] + b_mod[i]).reshape(1, N_MOD, D_MODEL)
        ms = (cond_s @ w_mod[i] + b_mod[i]).reshape(-1, N_MOD, D_MODEL)
        xp = post_norm(xp, swiglu(modulate(xp, mp, 0), ffn_w_in[i, 0], ffn_w_out[i, 0]), mp, 0, ln_g[i, 0], ln_b[i, 0], 0.5)
        xs = post_norm(xs, swiglu(modulate(xs, ms, 0), ffn_w_in[i, 0], ffn_w_out[i, 0]), ms, 0, ln_g[i, 0], ln_b[i, 0], 0.5)
        if i % N_MIXERS == 0:
            j = i // N_MIXERS
            mw = (m_w_up[j], m_w_conv[j], m_b_conv[j], m_w_qk[j], m_w_v[j], m_w_gate[j], m_b_gate[j],
                  m_w_o[j], m_b_o[j], m_norm_g[j], m_skip[j], m_w_down[j])
            zC = jnp.zeros((bp, 2, M_HEADS, M_HEAD_DIM, M_HEAD_DIM), f32)
            zn = jnp.zeros((bp, 2, M_HEADS, M_HEAD_DIM), f32)
            zm = jnp.zeros((bp, 2, M_HEADS), f32)
            yp, Cp, n_p, m_p = mlstm_mixer(modulate(xp, mp, 1), *mw, zC, zn, zm)
            ys, _, _, _ = mlstm_mixer(modulate(xs, ms, 1), *mw, state_C[:, j], state_n[:, j], state_m[:, j])
            new_C.append(Cp); new_n.append(n_p); new_m.append(m_p)
        else:
            j = i // N_MIXERS
            cw = (cv_w_pw1[j], cv_b_pw1[j], cv_w_dw[j], cv_b_dw[j], cv_ln_g[j], cv_ln_b[j], cv_w_pw2[j], cv_b_pw2[j])
            yp = conformer_conv(modulate(xp, mp, 1), 1, *cw)
            ys = conformer_conv(modulate(xs, ms, 1), rows, *cw)
        xp = post_norm(xp, yp, mp, 1, ln_g[i, 1], ln_b[i, 1], 1.0)
        xs = post_norm(xs, ys, ms, 1, ln_g[i, 1], ln_b[i, 1], 1.0)
        xp = post_norm(xp, swiglu(modulate(xp, mp, 2), ffn_w_in[i, 1], ffn_w_out[i, 1]), mp, 2, ln_g[i, 2], ln_b[i, 2], 0.5)
        xs = post_norm(xs, swiglu(modulate(xs, ms, 2), ffn_w_in[i, 1], ffn_w_out[i, 1]), ms, 2, ln_g[i, 2], ln_b[i, 2], 0.5)
    new_C_arr = jnp.stack(new_C, 1)
    new_n_arr = jnp.stack(new_n, 1)
    new_m_arr = jnp.stack(new_m, 1)
    return (xp, xs, new_C_arr, new_n_arr, new_m_arr)
```

```python
import functools

import jax
import jax.numpy as jnp
from jax import lax
from jax.experimental import pallas as pl
from jax.experimental.pallas import tpu as pltpu

F32 = jnp.float32
BF16 = jnp.bfloat16

DEPTH = 2
GRID_W = 64
M_HEADS = 4
CHUNK = 128
N_SUB = 3
N_MOD = 3 * N_SUB
ALPHA = (2 * DEPTH) ** 0.25
LN_EPS = 1e-5

SUBLANES = 8
LANES = 128
VMEM_LIMIT = 56 * 1024 * 1024

COND_ROWS = 8


def _resident(shape):
    nd = len(shape)
    return pl.BlockSpec(shape, lambda *_: (0,) * nd, pipeline_mode=pl.Buffered(1))


def _params(sem):
    return pltpu.CompilerParams(dimension_semantics=sem, vmem_limit_bytes=VMEM_LIMIT)


def _sigmoid(x):
    return 1.0 / (1.0 + jnp.exp(-x))


def _silu(x):
    return x * _sigmoid(x)


def _log_sigmoid(x):
    return jnp.minimum(x, 0.0) - jnp.log1p(jnp.exp(-jnp.abs(x)))


def _layer_norm(x, g, b):
    mu = jnp.mean(x, axis=-1, keepdims=True)
    xc = x - mu
    var = jnp.mean(xc * xc, axis=-1, keepdims=True)
    return xc * lax.rsqrt(var + LN_EPS) * g + b


def _modulate(x, mod_ref, s):
    shift = mod_ref[0, 3 * s:3 * s + 1, :]
    scale = mod_ref[0, 3 * s + 1:3 * s + 2, :]
    return x * (1.0 + scale) + shift


def _post_norm(x, y, mod_ref, s, g_ref, b_ref, res_w):
    gate = mod_ref[0, 3 * s + 2:3 * s + 3, :]
    return _layer_norm(ALPHA * x + (res_w * gate) * y, g_ref[...], b_ref[...])


def _dot(a, b):
    return jnp.dot(a, b, preferred_element_type=F32)


def _mod_kernel(c_ref, w_ref, b_ref, o_ref):
    cond = _silu(c_ref[...])
    o_ref[0] = _dot(cond.astype(BF16), w_ref[0].astype(BF16)) + b_ref[0]


def _mod_call(cond, w_mod, b_mod):
    depth, d, n = w_mod.shape
    tn = n // 8
    return pl.pallas_call(
        _mod_kernel,
        grid=(depth, n // tn),
        in_specs=[
            pl.BlockSpec((COND_ROWS, d), lambda l, j: (0, 0)),
            pl.BlockSpec((1, d, tn), lambda l, j: (l, 0, j)),
            pl.BlockSpec((1, 1, tn), lambda l, j: (l, 0, j)),
        ],
        out_specs=pl.BlockSpec((1, COND_ROWS, tn), lambda l, j: (l, 0, j)),
        out_shape=jax.ShapeDtypeStruct((depth, COND_ROWS, n), F32),
        compiler_params=_params(("arbitrary", "arbitrary")),
        name="adaln_mod",
    )(cond, w_mod, b_mod.reshape(depth, 1, n))


def _ffn_kernel(x_ref, mod_ref, win_ref, wout_ref, g_ref, b_ref, o_ref, *, s, f):
    x = x_ref[...]
    xm = _modulate(x, mod_ref, s).astype(BF16)
    au = _dot(xm, win_ref[...])
    a = au[:, :f]
    u = au[:, f:]
    hmid = (_silu(a) * u).astype(BF16)
    y = _dot(hmid, wout_ref[...])
    o_ref[...] = _post_norm(x, y, mod_ref, s, g_ref, b_ref, 0.5)


def _ffn_call(x, mod, w_in, w_out, g, b, *, s, tm, group_rows):
    n, d = x.shape
    f = w_out.shape[0]
    tiles_per_group = group_rows // tm
    row = pl.BlockSpec((tm, d), lambda i: (i, 0))
    return pl.pallas_call(
        functools.partial(_ffn_kernel, s=s, f=f),
        grid=(n // tm,),
        in_specs=[
            row,
            pl.BlockSpec((1, N_MOD, d), lambda i: (i // tiles_per_group, 0, 0)),
            _resident(w_in.shape),
            _resident(w_out.shape),
            _resident(g.shape),
            _resident(b.shape),
        ],
        out_specs=row,
        out_shape=jax.ShapeDtypeStruct((n, d), F32),
        compiler_params=_params(("arbitrary",)),
        name=f"ffn_s{s}",
    )(x, mod, w_in, w_out, g, b)


def _mup_kernel(x_ref, mod_ref, w_ref, xm_ref, z_ref, *, di):
    xmod = _modulate(x_ref[...], mod_ref, 1).astype(BF16)
    r = _dot(xmod, w_ref[...])
    xm_ref[...] = r[:, :di]
    z_ref[...] = r[:, di:]


def _mup_call(x, mod, w_up, *, tm, group_rows):
    n, d = x.shape
    di = w_up.shape[1] // 2
    tiles_per_group = group_rows // tm
    out = pl.BlockSpec((tm, di), lambda i: (i, 0))
    return pl.pallas_call(
        functools.partial(_mup_kernel, di=di),
        grid=(n // tm,),
        in_specs=[
            pl.BlockSpec((tm, d), lambda i: (i, 0)),
            pl.BlockSpec((1, N_MOD, d), lambda i: (i // tiles_per_group, 0, 0)),
            _resident(w_up.shape),
        ],
        out_specs=[out, out],
        out_shape=[jax.ShapeDtypeStruct((n, di), F32)] * 2,
        compiler_params=_params(("arbitrary",)),
        name="mlstm_up",
    )(x, mod, w_up)


def _mqk_kernel(xm_ref, prev_ref, next_ref, wc_ref, bc_ref, wqk_ref, wg_ref,
                xc_ref, q_ref, k_ref, pre_ref, ext_ref, *, tm, di, conv_w, prompt_tiles,
                prompt_seq, sample_seq):
    i = pl.program_id(0)
    halo = SUBLANES
    ext_ref[0:halo, :] = prev_ref[...]
    ext_ref[halo:halo + tm, :] = xm_ref[...]
    ext_ref[halo + tm:2 * halo + tm, :] = next_ref[...]
    seq = jnp.where(i < prompt_tiles, prompt_seq, sample_seq)
    row = i * tm + lax.broadcasted_iota(jnp.int32, (tm, 1), 0)
    pos = row & (seq - 1)
    acc = jnp.broadcast_to(bc_ref[...], (tm, di))
    for j in range(conv_w):
        off = j - conv_w // 2
        tap = ext_ref[halo + off:halo + off + tm, :]
        valid = (pos + off >= 0) & (pos + off < seq)
        acc = acc + jnp.where(valid, tap, 0.0) * wc_ref[j:j + 1, :]
    xc = _silu(acc)
    xc_ref[...] = xc
    qk = _dot(xc.astype(BF16), wqk_ref[...]).astype(BF16)
    q_ref[...] = qk[:, :di]
    k_ref[...] = qk[:, di:]
    pre_ref[...] = _dot(qk, wg_ref[...])


def _mqk_call(xm, w_conv, b_conv, w_qk, wg_qk, *, tm, prompt_rows, prompt_seq, sample_seq):
    n, di = xm.shape
    halo_blocks = n // SUBLANES
    per_tile = tm // SUBLANES
    row = pl.BlockSpec((tm, di), lambda i: (i, 0))
    kern = functools.partial(
        _mqk_kernel, tm=tm, di=di, conv_w=w_conv.shape[0], prompt_tiles=prompt_rows // tm,
        prompt_seq=prompt_seq, sample_seq=sample_seq)
    return pl.pallas_call(
        kern,
        grid=(n // tm,),
        in_specs=[
            row,
            pl.BlockSpec((SUBLANES, di), lambda i: (jnp.maximum(i * per_tile - 1, 0), 0)),
            pl.BlockSpec((SUBLANES, di),
                         lambda i: (jnp.minimum((i + 1) * per_tile, halo_blocks - 1), 0)),
            _resident(w_conv.shape),
            _resident(b_conv.shape),
            _resident(w_qk.shape),
            _resident(wg_qk.shape),
        ],
        out_specs=[row, row, row, pl.BlockSpec((tm, LANES), lambda i: (i, 0))],
        out_shape=[
            jax.ShapeDtypeStruct((n, di), F32),
            jax.ShapeDtypeStruct((n, di), BF16),
            jax.ShapeDtypeStruct((n, di), BF16),
            jax.ShapeDtypeStruct((n, LANES), F32),
        ],
        scratch_shapes=[pltpu.VMEM((tm + 2 * SUBLANES, di), F32)],
        compiler_params=_params(("arbitrary",)),
        name="mlstm_conv_qk",
    )(xm, xm, xm, w_conv, b_conv, w_qk, wg_qk)


def _mvo_kernel(xm_ref, pre_ref, wv_ref, wo_ref, bo_ref, wg_ref, bg_ref, v_ref, o_ref, preo_ref):
    xm = xm_ref[...].astype(BF16)
    v = _dot(xm, wv_ref[...]).astype(BF16)
    v_ref[...] = v
    o_ref[...] = _sigmoid(_dot(xm, wo_ref[...]) + bo_ref[...])
    preo_ref[...] = pre_ref[...] + _dot(v, wg_ref[...]) + bg_ref[...]


def _mvo_call(xm, pre_qk, w_v, w_o, b_o, wg_v, bg, *, tm):
    n, di = xm.shape
    row = pl.BlockSpec((tm, di), lambda i: (i, 0))
    gate = pl.BlockSpec((tm, LANES), lambda i: (i, 0))
    return pl.pallas_call(
        _mvo_kernel,
        grid=(n // tm,),
        in_specs=[row, gate, _resident(w_v.shape), _resident(w_o.shape), _resident(b_o.shape),
                  _resident(wg_v.shape), _resident(bg.shape)],
        out_specs=[row, row, gate],
        out_shape=[
            jax.ShapeDtypeStruct((n, di), BF16),
            jax.ShapeDtypeStruct((n, di), F32),
            jax.ShapeDtypeStruct((n, LANES), F32),
        ],
        compiler_params=_params(("arbitrary",)),
        name="mlstm_v_ogate",
    )(xm, pre_qk, w_v, w_o, b_o, wg_v, bg)


def _mlstm_core_kernel(*refs, nc, dh, has_state, emit_state):
    refs = list(refs)
    q_ref, k_ref, v_ref, gc_ref, gr_ref, ng_ref = refs[:6]
    pos = 6
    if has_state:
        c0_ref, n0_ref, m0_ref = refs[pos:pos + 3]
        pos += 3
    h_ref = refs[pos]
    pos += 1
    if emit_state:
        cn_ref, nn_ref, mn_ref = refs[pos:pos + 3]
        pos += 3
    c_sc, n_sc, m_sc = refs[pos:pos + 3]

    d = pl.program_id(2)
    c = pl.program_id(3)
    L = CHUNK
    qscale = dh ** -0.5

    @pl.when(c == 0)
    def _():
        if has_state:
            c_sc[...] = c0_ref[0, 0, 0, 0]
            n_sc[...] = n0_ref[0, 0, 0, 0]
            m_sc[...] = m0_ref[0]
        else:
            c_sc[...] = jnp.zeros_like(c_sc)
            n_sc[...] = jnp.zeros_like(n_sc)
            m_sc[...] = jnp.zeros_like(m_sc)

    q = q_ref[...]
    k = k_ref[...]
    v = v_ref[...]
    gcol = gc_ref[0, 0]
    grow = gr_ref[0, 0]
    i_col = gcol[:, 0:1]
    f_col = _log_sigmoid(gcol[:, 1:2])
    i_row = grow[0:1, :]
    f_row = _log_sigmoid(grow[1:2, :])

    t_idx = lax.broadcasted_iota(jnp.int32, (L, L), 0)
    s_idx = lax.broadcasted_iota(jnp.int32, (L, L), 1)
    sgn = 1 - 2 * d
    causal = (t_idx - s_idx) * sgn >= 0
    causal_t = (s_idx - t_idx) * sgn >= 0
    b_col = jnp.sum(jnp.where(causal, f_row, 0.0), axis=1, keepdims=True)
    b_row = jnp.sum(jnp.where(causal_t, f_col, 0.0), axis=0, keepdims=True)

    m_prev = m_sc[:, 0:1]
    dmat = jnp.where(causal, b_col - b_row + i_row, -jnp.inf)
    inter = b_col + m_prev
    mr = jnp.maximum(jnp.max(dmat, axis=1, keepdims=True), inter)
    qk = lax.dot_general(q, k, (((1,), (1,)), ((), ())), preferred_element_type=F32)
    p = qk * (qscale * jnp.exp(dmat - mr))
    w_in = qscale * jnp.exp(inter - mr)
    c_prev = c_sc[...]
    n_prev = n_sc[...]
    num = _dot(p.astype(BF16), v) + w_in * _dot(q, c_prev.astype(BF16))
    qn = jnp.sum(q.astype(F32) * n_prev, axis=1, keepdims=True)
    den = jnp.sum(p, axis=1, keepdims=True) + w_in * qn
    h = num * (1.0 / jnp.maximum(jnp.abs(den), jnp.exp(-mr)))

    cc = c + d * (nc - 1 - 2 * c)
    rows = pl.ds(pl.multiple_of(cc * L, L), L)

    @pl.when(d == 0)
    def _():
        h_ref[rows, :] = h

    @pl.when(d == 1)
    def _():
        hs = h_ref[rows, :] + h
        mu = jnp.mean(hs, axis=-1, keepdims=True)
        hc = hs - mu
        var = jnp.mean(hc * hc, axis=-1, keepdims=True)
        h_ref[rows, :] = hc * lax.rsqrt(var + LN_EPS) * ng_ref[...]

    def _update():
        g = jnp.sum(f_row, axis=1, keepdims=True)
        lw = g - b_col + i_col
        m_new = jnp.maximum(g + m_prev, jnp.max(lw, axis=0, keepdims=True))
        decay = jnp.exp(g + m_prev - m_new)
        kw = k.astype(F32) * jnp.exp(lw - m_new)
        c_new = decay * c_prev + lax.dot_general(
            kw.astype(BF16), v, (((0,), (0,)), ((), ())), preferred_element_type=F32)
        n_new = decay * n_prev + jnp.sum(kw, axis=0, keepdims=True)
        c_sc[...] = c_new
        n_sc[...] = n_new
        m_sc[...] = jnp.broadcast_to(m_new, m_sc.shape)
        return c_new, n_new, m_new

    if emit_state:
        c_new, n_new, m_new = _update()

        @pl.when(c == nc - 1)
        def _():
            cn_ref[0, 0, 0, 0] = c_new
            nn_ref[0, 0, 0, 0] = n_new
            mn_ref[0] = jnp.broadcast_to(m_new, mn_ref.shape[1:])
    else:
        @pl.when(c < nc - 1)
        def _():
            _update()


def _mlstm_core_call(q, k, v, gcol, grow, norm_g, state, *, row0, batch, seq, emit_state):
    dh = q.shape[1] // M_HEADS
    nc = seq // CHUNK
    chunk0 = row0 // CHUNK
    has_state = state is not None

    def chunk_of(b, d, c):
        return chunk0 + b * nc + c + d * (nc - 1 - 2 * c)

    tok = pl.BlockSpec((CHUNK, dh), lambda b, h, d, c: (chunk_of(b, d, c), h))
    in_specs = [
        tok, tok, tok,
        pl.BlockSpec((1, 1, CHUNK, 2), lambda b, h, d, c: (d, h, chunk_of(b, d, c), 0)),
        pl.BlockSpec((1, 1, 2, CHUNK), lambda b, h, d, c: (d, h, 0, chunk_of(b, d, c))),
        pl.BlockSpec((1, dh), lambda b, h, d, c: (0, h)),
    ]
    args = [q, k, v, gcol, grow, norm_g]
    if has_state:
        c0, n0, m0 = state
        in_specs += [
            pl.BlockSpec((1, 1, 1, 1, dh, dh), lambda b, h, d, c: (b, 0, d, h, 0, 0)),
            pl.BlockSpec((1, 1, 1, 1, 1, dh), lambda b, h, d, c: (b, 0, d, h, 0, 0)),
            pl.BlockSpec((1, 1, LANES), lambda b, h, d, c: ((b * 2 + d) * M_HEADS + h, 0, 0)),
        ]
        args += [c0, n0, m0]
    out_specs = [pl.BlockSpec((seq, dh), lambda b, h, d, c: (b, h))]
    out_shape = [jax.ShapeDtypeStruct((batch * seq, M_HEADS * dh), F32)]
    if emit_state:
        out_specs += [
            pl.BlockSpec((1, 1, 1, 1, dh, dh), lambda b, h, d, c: (b, 0, d, h, 0, 0)),
            pl.BlockSpec((1, 1, 1, 1, 1, dh), lambda b, h, d, c: (b, 0, d, h, 0, 0)),
            pl.BlockSpec((1, 1, LANES), lambda b, h, d, c: ((b * 2 + d) * M_HEADS + h, 0, 0)),
        ]
        out_shape += [
            jax.ShapeDtypeStruct((batch, 1, 2, M_HEADS, dh, dh), F32),
            jax.ShapeDtypeStruct((batch, 1, 2, M_HEADS, 1, dh), F32),
            jax.ShapeDtypeStruct((batch * 2 * M_HEADS, 1, LANES), F32),
        ]
    kern = functools.partial(_mlstm_core_kernel, nc=nc, dh=dh, has_state=has_state,
                             emit_state=emit_state)
    return pl.pallas_call(
        kern,
        grid=(batch, M_HEADS, 2, nc),
        in_specs=in_specs,
        out_specs=out_specs,
        out_shape=out_shape,
        scratch_shapes=[pltpu.VMEM((dh, dh), F32), pltpu.VMEM((1, dh), F32),
                        pltpu.VMEM((1, LANES), F32)],
        compiler_params=_params(("arbitrary",) * 4),
        name="mlstm_core_state" if emit_state else "mlstm_core",
    )(*args)


def _mdown_kernel(x_ref, mod_ref, o_ref, hp_ref, hs_ref, xc_ref, z_ref, skip_ref, wd_ref,
                  g_ref, b_ref, out_ref, *, prompt_tiles):
    i = pl.program_id(0)
    hn = jnp.where(i < prompt_tiles, hp_ref[...], hs_ref[...])
    mixed = (o_ref[...] * hn + skip_ref[...] * xc_ref[...]) * _silu(z_ref[...])
    y = _dot(mixed.astype(BF16), wd_ref[...])
    out_ref[...] = _post_norm(x_ref[...], y, mod_ref, 1, g_ref, b_ref, 1.0)


def _mdown_call(x, mod, o, hn_p, hn_s, xc, z, skip, w_down, g, b, *, tm, group_rows):
    n, d = x.shape
    di = o.shape[1]
    prompt_tiles = hn_p.shape[0] // tm
    tiles_per_group = group_rows // tm
    row = pl.BlockSpec((tm, di), lambda i: (i, 0))
    xrow = pl.BlockSpec((tm, d), lambda i: (i, 0))
    return pl.pallas_call(
        functools.partial(_mdown_kernel, prompt_tiles=prompt_tiles),
        grid=(n // tm,),
        in_specs=[
            xrow,
            pl.BlockSpec((1, N_MOD, d), lambda i: (i // tiles_per_group, 0, 0)),
            row,
            pl.BlockSpec((tm, di), lambda i: (jnp.minimum(i, prompt_tiles - 1), 0)),
            pl.BlockSpec((tm, di), lambda i: (jnp.maximum(i - prompt_tiles, 0), 0)),
            row, row,
            _resident(skip.shape), _resident(w_down.shape), _resident(g.shape),
            _resident(b.shape),
        ],
        out_specs=xrow,
        out_shape=jax.ShapeDtypeStruct((n, d), F32),
        compiler_params=_params(("arbitrary",)),
        name="mlstm_down",
    )(x, mod, o, hn_p, hn_s, xc, z, skip, w_down, g, b)


def _conf_kernel(x_ref, mod_ref, w1_ref, b1_ref, wdw_ref, bdw_ref, lg_ref, lb_ref, w2_ref,
                 b2_ref, g_ref, b_ref, out_ref, pad_ref, *, tm, d, conv_w, prompt_tiles,
                 prompt_seq, sample_seq):
    i = pl.program_id(0)
    half = conv_w // 2
    pad = 2 * SUBLANES
    x = x_ref[...]
    xm = _modulate(x, mod_ref, 1).astype(BF16)
    ag = _dot(xm, w1_ref[...]) + b1_ref[...]
    glu = ag[:, :d] * _sigmoid(ag[:, d:])
    pad_ref[0:pad, :] = jnp.zeros((pad, d), F32)
    pad_ref[pad + tm:2 * pad + tm, :] = jnp.zeros((pad, d), F32)
    pad_ref[pad:pad + tm, :] = glu
    seq = jnp.where(i < prompt_tiles, prompt_seq, sample_seq)
    pos = lax.broadcasted_iota(jnp.int32, (tm, 1), 0) & (seq - 1)
    acc = jnp.broadcast_to(bdw_ref[...], (tm, d))
    for j in range(conv_w):
        off = j - half
        tap = pad_ref[pad + off:pad + off + tm, :]
        valid = (pos + off >= 0) & (pos + off < seq)
        acc = acc + jnp.where(valid, tap, 0.0) * wdw_ref[j:j + 1, :]
    hact = _silu(_layer_norm(acc, lg_ref[...], lb_ref[...]))
    y = _dot(hact.astype(BF16), w2_ref[...]) + b2_ref[...]
    out_ref[...] = _post_norm(x, y, mod_ref, 1, g_ref, b_ref, 1.0)


def _conf_call(x, mod, w1, b1, wdw, bdw, lg, lb, w2, b2, g, b, *, tm, group_rows, prompt_rows,
               prompt_seq, sample_seq):
    n, d = x.shape
    tiles_per_group = group_rows // tm
    row = pl.BlockSpec((tm, d), lambda i: (i, 0))
    kern = functools.partial(_conf_kernel, tm=tm, d=d, conv_w=wdw.shape[0],
                             prompt_tiles=prompt_rows // tm, prompt_seq=prompt_seq,
                             sample_seq=sample_seq)
    consts = [w1, b1, wdw, bdw, lg, lb, w2, b2, g, b]
    return pl.pallas_call(
        kern,
        grid=(n // tm,),
        in_specs=[row, pl.BlockSpec((1, N_MOD, d), lambda i: (i // tiles_per_group, 0, 0))]
        + [_resident(a.shape) for a in consts],
        out_specs=row,
        out_shape=jax.ShapeDtypeStruct((n, d), F32),
        scratch_shapes=[pltpu.VMEM((tm + 4 * SUBLANES, d), F32)],
        compiler_params=_params(("arbitrary",)),
        name="conformer_conv",
    )(x, mod, *consts)


def _row(a):
    return a.reshape(1, -1)


def kernel(x_prompt, x_sample, state_C, state_n, state_m, c, c_ctx, w_mod, b_mod, ln_g, ln_b, ffn_w_in, ffn_w_out, m_w_up, m_w_conv, m_b_conv, m_w_qk, m_w_v, m_w_gate, m_b_gate, m_w_o, m_b_o, m_norm_g, m_skip, m_w_down, cv_w_pw1, cv_b_pw1, cv_w_dw, cv_b_dw, cv_ln_g, cv_ln_b, cv_w_pw2, cv_b_pw2):
    bp, sp, d = x_prompt.shape
    bs, ss, _ = x_sample.shape
    n_p, n_s = bp * sp, bs * ss
    heads = M_HEADS
    group_rows = ss
    assert n_p == group_rows and sp & (sp - 1) == 0 and ss & (ss - 1) == 0
    assert 1 + bs <= COND_ROWS and GRID_W & (GRID_W - 1) == 0

    x = jnp.concatenate([x_prompt.reshape(n_p, d), x_sample.reshape(n_s, d)], axis=0)
    cond = jnp.concatenate([c_ctx[None], c, jnp.zeros((COND_ROWS - 1 - bs, d), F32)], axis=0)
    mod_all = _mod_call(cond, w_mod, b_mod).reshape(DEPTH, COND_ROWS, N_MOD, d)

    new_c = new_n = new_m = None
    for i in range(DEPTH):
        mod = mod_all[i]
        ffn = functools.partial(_ffn_call, tm=512, group_rows=group_rows)
        x = ffn(x, mod, ffn_w_in[i, 0].astype(BF16), ffn_w_out[i, 0].astype(BF16),
                _row(ln_g[i, 0]), _row(ln_b[i, 0]), s=0)
        j = i // 2
        if i % 2 == 0:
            di = m_w_v.shape[1]
            dh = di // heads
            xm, z = _mup_call(x, mod, m_w_up[j].astype(BF16), tm=512, group_rows=group_rows)
            n_gate = m_w_gate.shape[2]
            wg = jnp.pad(m_w_gate[j], ((0, 0), (0, LANES - n_gate))).astype(BF16)
            bg = jnp.pad(m_b_gate[j], (0, LANES - n_gate)).reshape(1, LANES)
            xc, q, k, pre_qk = _mqk_call(
                xm, m_w_conv[j], _row(m_b_conv[j]), m_w_qk[j].astype(BF16), wg[:2 * di],
                tm=256, prompt_rows=n_p, prompt_seq=sp, sample_seq=ss)
            v, o, pre = _mvo_call(xm, pre_qk, m_w_v[j].astype(BF16), m_w_o[j].astype(BF16),
                                  _row(m_b_o[j]), wg[2 * di:], bg, tm=256)
            gates = pre[:, :n_gate].reshape(n_p + n_s, 2, 2, heads)
            gcol = gates.transpose(1, 3, 0, 2)
            grow = gates.transpose(1, 3, 2, 0)
            norm_g = _row(m_norm_g[j])
            hn_p, cp, np_, mp = _mlstm_core_call(
                q, k, v, gcol, grow, norm_g, None, row0=0, batch=bp, seq=sp, emit_state=True)
            state = (state_C[:, j:j + 1],
                     state_n[:, j:j + 1].reshape(bs, 1, 2, heads, 1, dh),
                     jnp.broadcast_to(state_m[:, j].reshape(bs * 2 * heads, 1, 1),
                                      (bs * 2 * heads, 1, LANES)))
            (hn_s,) = _mlstm_core_call(
                q, k, v, gcol, grow, norm_g, state, row0=n_p, batch=bs, seq=ss,
                emit_state=False)
            x = _mdown_call(x, mod, o, hn_p, hn_s, xc, z, _row(m_skip[j]),
                            m_w_down[j].astype(BF16), _row(ln_g[i, 1]), _row(ln_b[i, 1]),
                            tm=256, group_rows=group_rows)
            new_c = cp
            new_n = np_.reshape(bp, 1, 2, heads, dh)
            new_m = mp[:, 0, 0].reshape(bp, 1, 2, heads)
        else:
            x = _conf_call(
                x, mod, cv_w_pw1[j].astype(BF16), _row(cv_b_pw1[j]), cv_w_dw[j],
                _row(cv_b_dw[j]), _row(cv_ln_g[j]), _row(cv_ln_b[j]), cv_w_pw2[j].astype(BF16),
                _row(cv_b_pw2[j]), _row(ln_g[i, 1]), _row(ln_b[i, 1]), tm=256,
                group_rows=group_rows, prompt_rows=n_p, prompt_seq=sp, sample_seq=GRID_W)
        x = ffn(x, mod, ffn_w_in[i, 1].astype(BF16), ffn_w_out[i, 1].astype(BF16),
                _row(ln_g[i, 2]), _row(ln_b[i, 2]), s=2)
    return (x[:n_p].reshape(bp, sp, d), x[n_p:].reshape(bs, ss, d), new_c, new_n, new_m)
```

```python
import functools

import jax
import jax.numpy as jnp
from jax import lax
from jax.experimental import pallas as pl
from jax.experimental.pallas import tpu as pltpu

F32 = jnp.float32
BF16 = jnp.bfloat16

DEPTH = 2
GRID_W = 64
M_HEADS = 4
N_SUB = 3
N_MOD = 3 * N_SUB
ALPHA = (2 * DEPTH) ** 0.25
LN_EPS = 1e-5

SUBLANES = 8
LANES = 128
VMEM_LIMIT = 56 * 1024 * 1024

COND_ROWS = 8

TM_FFN = 512
TM_UP = 512
TM_MIX = 256
MLSTM_CHUNK = 256


def _resident(shape):
    nd = len(shape)
    return pl.BlockSpec(shape, lambda *_: (0,) * nd, pipeline_mode=pl.Buffered(1))


def _params(sem):
    return pltpu.CompilerParams(dimension_semantics=sem, vmem_limit_bytes=VMEM_LIMIT)


def _sigmoid(x):
    return 1.0 / (1.0 + jnp.exp(-x))


def _silu(x):
    return x * _sigmoid(x)


def _log_sigmoid(x):
    return jnp.minimum(x, 0.0) - jnp.log1p(jnp.exp(-jnp.abs(x)))


def _layer_norm(x, g, b):
    mu = jnp.mean(x, axis=-1, keepdims=True)
    xc = x - mu
    var = jnp.mean(xc * xc, axis=-1, keepdims=True)
    return xc * lax.rsqrt(var + LN_EPS) * g + b


def _modulate(x, mod_ref, s):
    shift = mod_ref[0, 3 * s:3 * s + 1, :]
    scale = mod_ref[0, 3 * s + 1:3 * s + 2, :]
    return x * (1.0 + scale) + shift


def _post_norm(x, y, mod_ref, s, g_ref, b_ref, res_w):
    gate = mod_ref[0, 3 * s + 2:3 * s + 3, :]
    return _layer_norm(ALPHA * x + (res_w * gate) * y, g_ref[...], b_ref[...])


def _dot(a, b):
    return jnp.dot(a, b, preferred_element_type=F32)


def _mod_spec(d, tiles_per_group):
    return pl.BlockSpec((1, N_MOD, d), lambda i: (i // tiles_per_group, 0, 0))


def _group_specs(tm, width, prompt_tiles):
    return (pl.BlockSpec((tm, width), lambda i: (jnp.minimum(i, prompt_tiles - 1), 0)),
            pl.BlockSpec((tm, width), lambda i: (jnp.maximum(i - prompt_tiles, 0), 0)))


def _mod_kernel(c_ref, w_ref, b_ref, o_ref):
    cond = _silu(c_ref[...])
    o_ref[0] = _dot(cond.astype(BF16), w_ref[0].astype(BF16)) + b_ref[0]


def _mod_call(cond, w_mod, b_mod):
    depth, d, n = w_mod.shape
    tn = n // 8
    return pl.pallas_call(
        _mod_kernel,
        grid=(depth, n // tn),
        in_specs=[
            pl.BlockSpec((COND_ROWS, d), lambda l, j: (0, 0)),
            pl.BlockSpec((1, d, tn), lambda l, j: (l, 0, j)),
            pl.BlockSpec((1, 1, tn), lambda l, j: (l, 0, j)),
        ],
        out_specs=pl.BlockSpec((1, COND_ROWS, tn), lambda l, j: (l, 0, j)),
        out_shape=jax.ShapeDtypeStruct((depth, COND_ROWS, n), F32),
        compiler_params=_params(("arbitrary", "arbitrary")),
        name="adaln_mod",
    )(cond, w_mod, b_mod.reshape(depth, 1, n))


def _ffn_kernel(*refs, s, f, split_in, split_out, prompt_tiles):
    refs = list(refs)
    n_x = 2 if split_in else 1
    x_refs = refs[:n_x]
    mod_ref, win_ref, wout_ref, g_ref, b_ref = refs[n_x:n_x + 5]
    o_refs = refs[n_x + 5:]
    i = pl.program_id(0)
    if split_in:
        x = jnp.where(i < prompt_tiles, x_refs[0][...], x_refs[1][...])
    else:
        x = x_refs[0][...]
    xm = _modulate(x, mod_ref, s).astype(BF16)
    au = _dot(xm, win_ref[...])
    hmid = (_silu(au[:, :f]) * au[:, f:]).astype(BF16)
    y = _dot(hmid, wout_ref[...])
    res = _post_norm(x, y, mod_ref, s, g_ref, b_ref, 0.5)
    if split_out:
        @pl.when(i < prompt_tiles)
        def _():
            o_refs[0][...] = res

        @pl.when(i >= prompt_tiles)
        def _():
            o_refs[1][...] = res
    else:
        o_refs[0][...] = res


def _ffn_call(xs, mod, w_in, w_out, g, b, *, s, prompt_rows, group_rows, split_out=False):
    tm = TM_FFN
    split_in = len(xs) == 2
    n = sum(x.shape[0] for x in xs)
    d = xs[0].shape[1]
    f = w_out.shape[0]
    prompt_tiles = prompt_rows // tm
    row = pl.BlockSpec((tm, d), lambda i: (i, 0))
    pair = _group_specs(tm, d, prompt_tiles)
    if split_out:
        out_specs = list(pair)
        out_shape = [jax.ShapeDtypeStruct((prompt_rows, d), F32),
                     jax.ShapeDtypeStruct((n - prompt_rows, d), F32)]
    else:
        out_specs = [row]
        out_shape = [jax.ShapeDtypeStruct((n, d), F32)]
    kern = functools.partial(_ffn_kernel, s=s, f=f, split_in=split_in, split_out=split_out,
                             prompt_tiles=prompt_tiles)
    return pl.pallas_call(
        kern,
        grid=(n // tm,),
        in_specs=(list(pair) if split_in else [row]) + [
            _mod_spec(d, group_rows // tm),
            _resident(w_in.shape),
            _resident(w_out.shape),
            _resident(g.shape),
            _resident(b.shape),
        ],
        out_specs=out_specs,
        out_shape=out_shape,
        compiler_params=_params(("arbitrary",)),
        name=f"ffn_s{s}",
    )(*xs, mod, w_in, w_out, g, b)


def _mup_kernel(x_ref, mod_ref, w_ref, xm_ref, z_ref, *, di):
    xmod = _modulate(x_ref[...], mod_ref, 1).astype(BF16)
    r = _dot(xmod, w_ref[...])
    xm_ref[...] = r[:, :di]
    z_ref[...] = r[:, di:]


def _mup_call(x, mod, w_up, *, group_rows):
    tm = TM_UP
    n, d = x.shape
    di = w_up.shape[1] // 2
    out = pl.BlockSpec((tm, di), lambda i: (i, 0))
    return pl.pallas_call(
        functools.partial(_mup_kernel, di=di),
        grid=(n // tm,),
        in_specs=[
            pl.BlockSpec((tm, d), lambda i: (i, 0)),
            _mod_spec(d, group_rows // tm),
            _resident(w_up.shape),
        ],
        out_specs=[out, out],
        out_shape=[jax.ShapeDtypeStruct((n, di), F32)] * 2,
        compiler_params=_params(("arbitrary",)),
        name="mlstm_up",
    )(x, mod, w_up)


def _mqk_kernel(xm_ref, prev_ref, next_ref, wc_ref, bc_ref, wqk_ref, wg_ref,
                xc_ref, q_ref, k_ref, pre_ref, ext_ref, *, tm, di, conv_w, prompt_tiles,
                prompt_seq, sample_seq):
    i = pl.program_id(0)
    halo = SUBLANES
    seq = jnp.where(i < prompt_tiles, prompt_seq, sample_seq)
    starts_seq = ((i * tm) & (seq - 1)) == 0
    ends_seq = ((i * tm + tm) & (seq - 1)) == 0
    ext_ref[0:halo, :] = jnp.where(starts_seq, 0.0, prev_ref[...])
    ext_ref[halo:halo + tm, :] = xm_ref[...]
    ext_ref[halo + tm:2 * halo + tm, :] = jnp.where(ends_seq, 0.0, next_ref[...])
    acc = jnp.broadcast_to(bc_ref[...], (tm, di))
    for j in range(conv_w):
        off = j - conv_w // 2
        acc = acc + ext_ref[halo + off:halo + off + tm, :] * wc_ref[j:j + 1, :]
    xc = _silu(acc)
    xc_ref[...] = xc
    qk = _dot(xc.astype(BF16), wqk_ref[...]).astype(BF16)
    q_ref[...] = qk[:, :di]
    k_ref[...] = qk[:, di:]
    pre_ref[...] = _dot(qk, wg_ref[...])


def _mqk_call(xm, w_conv, b_conv, w_qk, wg_qk, *, prompt_rows, prompt_seq, sample_seq):
    tm = TM_MIX
    n, di = xm.shape
    assert prompt_seq % tm == 0 and sample_seq % tm == 0 and w_conv.shape[0] // 2 <= SUBLANES
    halo_blocks = n // SUBLANES
    per_tile = tm // SUBLANES
    row = pl.BlockSpec((tm, di), lambda i: (i, 0))
    kern = functools.partial(
        _mqk_kernel, tm=tm, di=di, conv_w=w_conv.shape[0], prompt_tiles=prompt_rows // tm,
        prompt_seq=prompt_seq, sample_seq=sample_seq)
    return pl.pallas_call(
        kern,
        grid=(n // tm,),
        in_specs=[
            row,
            pl.BlockSpec((SUBLANES, di), lambda i: (jnp.maximum(i * per_tile - 1, 0), 0)),
            pl.BlockSpec((SUBLANES, di),
                         lambda i: (jnp.minimum((i + 1) * per_tile, halo_blocks - 1), 0)),
            _resident(w_conv.shape),
            _resident(b_conv.shape),
            _resident(w_qk.shape),
            _resident(wg_qk.shape),
        ],
        out_specs=[row, row, row, pl.BlockSpec((tm, LANES), lambda i: (i, 0))],
        out_shape=[
            jax.ShapeDtypeStruct((n, di), F32),
            jax.ShapeDtypeStruct((n, di), BF16),
            jax.ShapeDtypeStruct((n, di), BF16),
            jax.ShapeDtypeStruct((n, LANES), F32),
        ],
        scratch_shapes=[pltpu.VMEM((tm + 2 * SUBLANES, di), F32)],
        compiler_params=_params(("arbitrary",)),
        name="mlstm_conv_qk",
    )(xm, xm, xm, w_conv, b_conv, w_qk, wg_qk)


def _mvo_kernel(xm_ref, pre_ref, wv_ref, wo_ref, bo_ref, wg_ref, bg_ref, v_ref, o_ref, preo_ref):
    xm = xm_ref[...].astype(BF16)
    v = _dot(xm, wv_ref[...]).astype(BF16)
    v_ref[...] = v
    o_ref[...] = _sigmoid(_dot(xm, wo_ref[...]) + bo_ref[...])
    preo_ref[...] = pre_ref[...] + _dot(v, wg_ref[...]) + bg_ref[...]


def _mvo_call(xm, pre_qk, w_v, w_o, b_o, wg_v, bg):
    tm = TM_MIX
    n, di = xm.shape
    row = pl.BlockSpec((tm, di), lambda i: (i, 0))
    gate = pl.BlockSpec((tm, LANES), lambda i: (i, 0))
    return pl.pallas_call(
        _mvo_kernel,
        grid=(n // tm,),
        in_specs=[row, gate, _resident(w_v.shape), _resident(w_o.shape), _resident(b_o.shape),
                  _resident(wg_v.shape), _resident(bg.shape)],
        out_specs=[row, row, gate],
        out_shape=[
            jax.ShapeDtypeStruct((n, di), BF16),
            jax.ShapeDtypeStruct((n, di), F32),
            jax.ShapeDtypeStruct((n, LANES), F32),
        ],
        compiler_params=_params(("arbitrary",)),
        name="mlstm_v_ogate",
    )(xm, pre_qk, w_v, w_o, b_o, wg_v, bg)


def _mlstm_core_kernel(*refs, nc, dh, chunk, has_state, emit_state):
    refs = list(refs)
    q_ref, k_ref, v_ref, g_ref = refs[:4]
    pos = 4
    if has_state:
        c0_ref, n0_ref, m0_ref = refs[pos:pos + 3]
        pos += 3
    h_ref = refs[pos]
    pos += 1
    if emit_state:
        cn_ref, nn_ref, mn_ref = refs[pos:pos + 3]
        pos += 3
    c_sc, n_sc, m_sc = refs[pos:pos + 3]

    d = pl.program_id(1)
    c = pl.program_id(2)
    L = chunk
    qscale = dh ** -0.5

    @pl.when(c == 0)
    def _():
        if has_state:
            c_sc[...] = c0_ref[0, 0, 0]
            n_sc[...] = n0_ref[0, 0, 0]
            m_sc[...] = m0_ref[0]
        else:
            c_sc[...] = jnp.zeros_like(c_sc)
            n_sc[...] = jnp.zeros_like(n_sc)
            m_sc[...] = jnp.zeros_like(m_sc)

    gates = g_ref[...]
    gates_t = gates.T
    fwd = d == 0
    t_idx = lax.broadcasted_iota(jnp.int32, (L, L), 0)
    s_idx = lax.broadcasted_iota(jnp.int32, (L, L), 1)
    sgn = 1 - 2 * d
    causal = (t_idx - s_idx) * sgn >= 0
    causal_t = (s_idx - t_idx) * sgn >= 0

    for h in range(M_HEADS):
        lanes = slice(h * dh, (h + 1) * dh)
        li, lf = h, M_HEADS + h
        bi, bf = 2 * M_HEADS + h, 3 * M_HEADS + h
        i_col = jnp.where(fwd, gates[:, li:li + 1], gates[:, bi:bi + 1])
        f_col = _log_sigmoid(jnp.where(fwd, gates[:, lf:lf + 1], gates[:, bf:bf + 1]))
        i_row = jnp.where(fwd, gates_t[li:li + 1, :], gates_t[bi:bi + 1, :])
        f_row = _log_sigmoid(jnp.where(fwd, gates_t[lf:lf + 1, :], gates_t[bf:bf + 1, :]))
        b_col = jnp.sum(jnp.where(causal, f_row, 0.0), axis=1, keepdims=True)
        b_row = jnp.sum(jnp.where(causal_t, f_col, 0.0), axis=0, keepdims=True)

        q = q_ref[:, lanes]
        k = k_ref[:, lanes]
        v = v_ref[:, lanes]
        m_prev = m_sc[:, h:h + 1]
        c_prev = c_sc[h]
        n_prev = n_sc[h:h + 1, :]
        dmat = jnp.where(causal, b_col - b_row + i_row, -jnp.inf)
        inter = b_col + m_prev
        mr = jnp.maximum(jnp.max(dmat, axis=1, keepdims=True), inter)
        qk = lax.dot_general(q, k, (((1,), (1,)), ((), ())), preferred_element_type=F32)
        p = qk * (qscale * jnp.exp(dmat - mr))
        w_in = qscale * jnp.exp(inter - mr)
        num = _dot(p.astype(BF16), v) + w_in * _dot(q, c_prev.astype(BF16))
        qn = jnp.sum(q.astype(F32) * n_prev, axis=1, keepdims=True)
        den = jnp.sum(p, axis=1, keepdims=True) + w_in * qn
        h_ref[0, :, lanes] = num * (1.0 / jnp.maximum(jnp.abs(den), jnp.exp(-mr)))

        def _update():
            g = jnp.sum(f_row, axis=1, keepdims=True)
            lw = g - b_col + i_col
            m_new = jnp.maximum(g + m_prev, jnp.max(lw, axis=0, keepdims=True))
            decay = jnp.exp(g + m_prev - m_new)
            kw = k.astype(F32) * jnp.exp(lw - m_new)
            c_new = decay * c_prev + lax.dot_general(
                kw.astype(BF16), v, (((0,), (0,)), ((), ())), preferred_element_type=F32)
            n_new = decay * n_prev + jnp.sum(kw, axis=0, keepdims=True)
            c_sc[h] = c_new
            n_sc[h:h + 1, :] = n_new
            m_sc[:, h:h + 1] = m_new

        if emit_state:
            _update()
        else:
            pl.when(c < nc - 1)(_update)

    if emit_state:
        @pl.when(c == nc - 1)
        def _():
            cn_ref[0, 0, 0] = c_sc[...]
            nn_ref[0, 0, 0] = n_sc[...]
            mn_ref[0] = m_sc[...]


def _mlstm_core_call(q, k, v, pre, state, *, row0, batch, seq, emit_state):
    di = q.shape[1]
    dh = di // M_HEADS
    chunk = min(MLSTM_CHUNK, seq)
    nc = seq // chunk
    chunk0 = row0 // chunk
    has_state = state is not None

    def chunk_of(b, d, c):
        return b * nc + c + d * (nc - 1 - 2 * c)

    tok = pl.BlockSpec((chunk, di), lambda b, d, c: (chunk0 + chunk_of(b, d, c), 0))
    in_specs = [tok, tok, tok,
                pl.BlockSpec((chunk, LANES), lambda b, d, c: (chunk0 + chunk_of(b, d, c), 0))]
    args = [q, k, v, pre]
    state_c = pl.BlockSpec((1, 1, 1, M_HEADS, dh, dh), lambda b, d, c: (b, 0, d, 0, 0, 0))
    state_n = pl.BlockSpec((1, 1, 1, M_HEADS, dh), lambda b, d, c: (b, 0, d, 0, 0))
    state_m = pl.BlockSpec((1, 1, LANES), lambda b, d, c: (b * 2 + d, 0, 0))
    if has_state:
        in_specs += [state_c, state_n, state_m]
        args += list(state)
    out_specs = [pl.BlockSpec((1, chunk, di), lambda b, d, c: (d, chunk_of(b, d, c), 0))]
    out_shape = [jax.ShapeDtypeStruct((2, batch * seq, di), F32)]
    if emit_state:
        out_specs += [state_c, state_n, state_m]
        out_shape += [
            jax.ShapeDtypeStruct((batch, 1, 2, M_HEADS, dh, dh), F32),
            jax.ShapeDtypeStruct((batch, 1, 2, M_HEADS, dh), F32),
            jax.ShapeDtypeStruct((batch * 2, 1, LANES), F32),
        ]
    kern = functools.partial(_mlstm_core_kernel, nc=nc, dh=dh, chunk=chunk,
                             has_state=has_state, emit_state=emit_state)
    return pl.pallas_call(
        kern,
        grid=(batch, 2, nc),
        in_specs=in_specs,
        out_specs=out_specs,
        out_shape=out_shape,
        scratch_shapes=[pltpu.VMEM((M_HEADS, dh, dh), F32), pltpu.VMEM((M_HEADS, dh), F32),
                        pltpu.VMEM((1, LANES), F32)],
        compiler_params=_params(("arbitrary",) * 3),
        name="mlstm_core_state" if emit_state else "mlstm_core",
    )(*args)


def _mdown_kernel(x_ref, mod_ref, o_ref, hpf_ref, hpb_ref, hsf_ref, hsb_ref, xc_ref, z_ref,
                  ng_ref, skip_ref, wd_ref, g_ref, b_ref, out_ref, *, prompt_tiles, dh):
    i = pl.program_id(0)
    hsum = jnp.where(i < prompt_tiles, hpf_ref[0] + hpb_ref[0], hsf_ref[0] + hsb_ref[0])
    parts = []
    for h in range(M_HEADS):
        hh = hsum[:, h * dh:(h + 1) * dh]
        mu = jnp.mean(hh, axis=-1, keepdims=True)
        hc = hh - mu
        var = jnp.mean(hc * hc, axis=-1, keepdims=True)
        parts.append(hc * lax.rsqrt(var + LN_EPS))
    hn = jnp.concatenate(parts, axis=-1) * ng_ref[...]
    mixed = (o_ref[...] * hn + skip_ref[...] * xc_ref[...]) * _silu(z_ref[...])
    y = _dot(mixed.astype(BF16), wd_ref[...])
    out_ref[...] = _post_norm(x_ref[...], y, mod_ref, 1, g_ref, b_ref, 1.0)


def _mdown_call(x, mod, o, h_p, h_s, xc, z, norm_g, skip, w_down, g, b, *, group_rows):
    tm = TM_MIX
    n, d = x.shape
    di = o.shape[1]
    prompt_tiles = h_p.shape[1] // tm
    row = pl.BlockSpec((tm, di), lambda i: (i, 0))
    xrow = pl.BlockSpec((tm, d), lambda i: (i, 0))

    def hspec(direction, first_group):
        if first_group:
            return pl.BlockSpec((1, tm, di),
                                lambda i: (direction, jnp.minimum(i, prompt_tiles - 1), 0))
        return pl.BlockSpec((1, tm, di),
                            lambda i: (direction, jnp.maximum(i - prompt_tiles, 0), 0))

    consts = [norm_g, skip, w_down, g, b]
    return pl.pallas_call(
        functools.partial(_mdown_kernel, prompt_tiles=prompt_tiles, dh=di // M_HEADS),
        grid=(n // tm,),
        in_specs=[xrow, _mod_spec(d, group_rows // tm), row,
                  hspec(0, True), hspec(1, True), hspec(0, False), hspec(1, False), row, row]
        + [_resident(a.shape) for a in consts],
        out_specs=xrow,
        out_shape=jax.ShapeDtypeStruct((n, d), F32),
        compiler_params=_params(("arbitrary",)),
        name="mlstm_down",
    )(x, mod, o, h_p, h_p, h_s, h_s, xc, z, *consts)


def _conf_kernel(x_ref, mod_ref, w1_ref, b1_ref, wdw_ref, bdw_ref, lg_ref, lb_ref, w2_ref,
                 b2_ref, g_ref, b_ref, out_ref, pad_ref, *, tm, d, conv_w, prompt_tiles,
                 prompt_seq, sample_seq):
    i = pl.program_id(0)
    half = conv_w // 2
    pad = 2 * SUBLANES
    x = x_ref[...]
    xm = _modulate(x, mod_ref, 1).astype(BF16)
    ag = _dot(xm, w1_ref[...]) + b1_ref[...]
    glu = ag[:, :d] * _sigmoid(ag[:, d:])
    pad_ref[0:pad, :] = jnp.zeros((pad, d), F32)
    pad_ref[pad + tm:2 * pad + tm, :] = jnp.zeros((pad, d), F32)
    pad_ref[pad:pad + tm, :] = glu
    seq = jnp.where(i < prompt_tiles, prompt_seq, sample_seq)
    pos = lax.broadcasted_iota(jnp.int32, (tm, 1), 0) & (seq - 1)
    acc = jnp.broadcast_to(bdw_ref[...], (tm, d))
    for j in range(conv_w):
        off = j - half
        tap = pad_ref[pad + off:pad + off + tm, :]
        valid = (pos + off >= 0) & (pos + off < seq)
        acc = acc + jnp.where(valid, tap, 0.0) * wdw_ref[j:j + 1, :]
    hact = _silu(_layer_norm(acc, lg_ref[...], lb_ref[...]))
    y = _dot(hact.astype(BF16), w2_ref[...]) + b2_ref[...]
    out_ref[...] = _post_norm(x, y, mod_ref, 1, g_ref, b_ref, 1.0)


def _conf_call(x, mod, w1, b1, wdw, bdw, lg, lb, w2, b2, g, b, *, group_rows, prompt_rows,
               prompt_seq, sample_seq):
    tm = TM_MIX
    n, d = x.shape
    assert tm % prompt_seq == 0 and tm % sample_seq == 0
    row = pl.BlockSpec((tm, d), lambda i: (i, 0))
    kern = functools.partial(_conf_kernel, tm=tm, d=d, conv_w=wdw.shape[0],
                             prompt_tiles=prompt_rows // tm, prompt_seq=prompt_seq,
                             sample_seq=sample_seq)
    consts = [w1, b1, wdw, bdw, lg, lb, w2, b2, g, b]
    return pl.pallas_call(
        kern,
        grid=(n // tm,),
        in_specs=[row, _mod_spec(d, group_rows // tm)] + [_resident(a.shape) for a in consts],
        out_specs=row,
        out_shape=jax.ShapeDtypeStruct((n, d), F32),
        scratch_shapes=[pltpu.VMEM((tm + 4 * SUBLANES, d), F32)],
        compiler_params=_params(("arbitrary",)),
        name="conformer_conv",
    )(x, mod, *consts)


def _row(a):
    return a.reshape(1, -1)


def kernel(x_prompt, x_sample, state_C, state_n, state_m, c, c_ctx, w_mod, b_mod, ln_g, ln_b, ffn_w_in, ffn_w_out, m_w_up, m_w_conv, m_b_conv, m_w_qk, m_w_v, m_w_gate, m_b_gate, m_w_o, m_b_o, m_norm_g, m_skip, m_w_down, cv_w_pw1, cv_b_pw1, cv_w_dw, cv_b_dw, cv_ln_g, cv_ln_b, cv_w_pw2, cv_b_pw2):
    bp, sp, d = x_prompt.shape
    bs, ss, _ = x_sample.shape
    n_p, n_s = bp * sp, bs * ss
    heads = M_HEADS
    group_rows = ss
    assert n_p == group_rows and sp & (sp - 1) == 0 and ss & (ss - 1) == 0
    assert 1 + bs <= COND_ROWS and GRID_W & (GRID_W - 1) == 0
    assert m_w_up.shape[0] == 1 and 4 * heads <= LANES

    cond = jnp.concatenate([c_ctx[None], c, jnp.zeros((COND_ROWS - 1 - bs, d), F32)], axis=0)
    mod_all = _mod_call(cond, w_mod, b_mod).reshape(DEPTH, COND_ROWS, N_MOD, d)

    xs = (x_prompt.reshape(n_p, d), x_sample.reshape(n_s, d))
    new_c = new_n = new_m = None
    for i in range(DEPTH):
        mod = mod_all[i]
        ffn = functools.partial(_ffn_call, prompt_rows=n_p, group_rows=group_rows)
        (x,) = ffn(xs, mod, ffn_w_in[i, 0].astype(BF16), ffn_w_out[i, 0].astype(BF16),
                   _row(ln_g[i, 0]), _row(ln_b[i, 0]), s=0)
        j = i // 2
        if i % 2 == 0:
            di = m_w_v.shape[1]
            xm, z = _mup_call(x, mod, m_w_up[j].astype(BF16), group_rows=group_rows)
            n_gate = m_w_gate.shape[2]
            wg = jnp.pad(m_w_gate[j], ((0, 0), (0, LANES - n_gate))).astype(BF16)
            bg = jnp.pad(m_b_gate[j], (0, LANES - n_gate)).reshape(1, LANES)
            xc, q, k, pre_qk = _mqk_call(
                xm, m_w_conv[j], _row(m_b_conv[j]), m_w_qk[j].astype(BF16), wg[:2 * di],
                prompt_rows=n_p, prompt_seq=sp, sample_seq=ss)
            v, o, pre = _mvo_call(xm, pre_qk, m_w_v[j].astype(BF16), m_w_o[j].astype(BF16),
                                  _row(m_b_o[j]), wg[2 * di:], bg)
            h_p, new_c, new_n, m_p = _mlstm_core_call(
                q, k, v, pre, None, row0=0, batch=bp, seq=sp, emit_state=True)
            m0 = jnp.pad(state_m[:, j].reshape(bs * 2, 1, heads),
                         ((0, 0), (0, 0), (0, LANES - heads)))
            (h_s,) = _mlstm_core_call(
                q, k, v, pre, (state_C[:, j:j + 1], state_n[:, j:j + 1], m0),
                row0=n_p, batch=bs, seq=ss, emit_state=False)
            x = _mdown_call(x, mod, o, h_p, h_s, xc, z, _row(m_norm_g[j]), _row(m_skip[j]),
                            m_w_down[j].astype(BF16), _row(ln_g[i, 1]), _row(ln_b[i, 1]),
                            group_rows=group_rows)
            new_m = m_p[:, 0, :heads].reshape(bp, 1, 2, heads)
        else:
            x = _conf_call(
                x, mod, cv_w_pw1[j].astype(BF16), _row(cv_b_pw1[j]), cv_w_dw[j],
                _row(cv_b_dw[j]), _row(cv_ln_g[j]), _row(cv_ln_b[j]), cv_w_pw2[j].astype(BF16),
                _row(cv_b_pw2[j]), _row(ln_g[i, 1]), _row(ln_b[i, 1]),
                group_rows=group_rows, prompt_rows=n_p, prompt_seq=sp, sample_seq=GRID_W)
        xs = ffn((x,), mod, ffn_w_in[i, 1].astype(BF16), ffn_w_out[i, 1].astype(BF16),
                 _row(ln_g[i, 2]), _row(ln_b[i, 2]), s=2, split_out=(i == DEPTH - 1))
    y_p, y_s = xs
    return (y_p.reshape(bp, sp, d), y_s.reshape(bs, ss, d), new_c, new_n, new_m)
```

```python
import functools

import jax
import jax.numpy as jnp
from jax import lax
from jax.experimental import pallas as pl
from jax.experimental.pallas import tpu as pltpu

F32 = jnp.float32
BF16 = jnp.bfloat16

DEPTH = 2
GRID_W = 64
M_HEADS = 4
N_SUB = 3
N_MOD = 3 * N_SUB
ALPHA = (2 * DEPTH) ** 0.25
LN_EPS = 1e-5

SUBLANES = 8
LANES = 128
VMEM_LIMIT = 56 * 1024 * 1024

COND_ROWS = 8

TM_FFN = 512
TM_UP = 512
TM_MIX = 256
MLSTM_CHUNK = 256


def _resident(shape):
    nd = len(shape)
    return pl.BlockSpec(shape, lambda *_: (0,) * nd, pipeline_mode=pl.Buffered(1))


def _params(sem):
    return pltpu.CompilerParams(dimension_semantics=sem, vmem_limit_bytes=VMEM_LIMIT)


def _sigmoid(x):
    return 1.0 / (1.0 + jnp.exp(-x))


def _silu(x):
    return x * _sigmoid(x)


def _log_sigmoid(x):
    return jnp.minimum(x, 0.0) - jnp.log1p(jnp.exp(-jnp.abs(x)))


def _layer_norm(x, g, b):
    mu = jnp.mean(x, axis=-1, keepdims=True)
    xc = x - mu
    var = jnp.mean(xc * xc, axis=-1, keepdims=True)
    return xc * lax.rsqrt(var + LN_EPS) * g + b


def _modulate(x, mod_ref, s):
    shift = mod_ref[0, 3 * s:3 * s + 1, :]
    scale = mod_ref[0, 3 * s + 1:3 * s + 2, :]
    return x * (1.0 + scale) + shift


def _post_norm(x, y, mod_ref, s, g_ref, b_ref, res_w):
    gate = mod_ref[0, 3 * s + 2:3 * s + 3, :]
    return _layer_norm(ALPHA * x + (res_w * gate) * y, g_ref[...], b_ref[...])


def _dot(a, b):
    return jnp.dot(a, b, preferred_element_type=F32)


def _mod_spec(d, tiles_per_group):
    return pl.BlockSpec((1, N_MOD, d), lambda i: (i // tiles_per_group, 0, 0))


def _group_specs(tm, width, prompt_tiles):
    return (pl.BlockSpec((tm, width), lambda i: (jnp.minimum(i, prompt_tiles - 1), 0)),
            pl.BlockSpec((tm, width), lambda i: (jnp.maximum(i - prompt_tiles, 0), 0)))


def _mod_kernel(c_ref, w_ref, b_ref, o_ref):
    cond = _silu(c_ref[...])
    o_ref[0] = _dot(cond.astype(BF16), w_ref[0].astype(BF16)) + b_ref[0]


def _mod_call(cond, w_mod, b_mod):
    depth, d, n = w_mod.shape
    tn = n // 8
    return pl.pallas_call(
        _mod_kernel,
        grid=(depth, n // tn),
        in_specs=[
            pl.BlockSpec((COND_ROWS, d), lambda l, j: (0, 0)),
            pl.BlockSpec((1, d, tn), lambda l, j: (l, 0, j)),
            pl.BlockSpec((1, 1, tn), lambda l, j: (l, 0, j)),
        ],
        out_specs=pl.BlockSpec((1, COND_ROWS, tn), lambda l, j: (l, 0, j)),
        out_shape=jax.ShapeDtypeStruct((depth, COND_ROWS, n), F32),
        compiler_params=_params(("arbitrary", "arbitrary")),
        name="adaln_mod",
    )(cond, w_mod, b_mod.reshape(depth, 1, n))


def _ffn_kernel(*refs, s, f, split_in, split_out, prompt_tiles):
    refs = list(refs)
    n_x = 2 if split_in else 1
    x_refs = refs[:n_x]
    mod_ref, win_ref, wout_ref, g_ref, b_ref = refs[n_x:n_x + 5]
    o_refs = refs[n_x + 5:]
    i = pl.program_id(0)
    if split_in:
        x = jnp.where(i < prompt_tiles, x_refs[0][...], x_refs[1][...])
    else:
        x = x_refs[0][...]
    xm = _modulate(x, mod_ref, s).astype(BF16)
    au = _dot(xm, win_ref[...])
    hmid = (_silu(au[:, :f]) * au[:, f:]).astype(BF16)
    y = _dot(hmid, wout_ref[...])
    res = _post_norm(x, y, mod_ref, s, g_ref, b_ref, 0.5)
    if split_out:
        @pl.when(i < prompt_tiles)
        def _():
            o_refs[0][...] = res

        @pl.when(i >= prompt_tiles)
        def _():
            o_refs[1][...] = res
    else:
        o_refs[0][...] = res


def _ffn_call(xs, mod, w_in, w_out, g, b, *, s, prompt_rows, group_rows, split_out=False):
    tm = TM_FFN
    split_in = len(xs) == 2
    n = sum(x.shape[0] for x in xs)
    d = xs[0].shape[1]
    f = w_out.shape[0]
    prompt_tiles = prompt_rows // tm
    row = pl.BlockSpec((tm, d), lambda i: (i, 0))
    pair = _group_specs(tm, d, prompt_tiles)
    if split_out:
        out_specs = list(pair)
        out_shape = [jax.ShapeDtypeStruct((prompt_rows, d), F32),
                     jax.ShapeDtypeStruct((n - prompt_rows, d), F32)]
    else:
        out_specs = [row]
        out_shape = [jax.ShapeDtypeStruct((n, d), F32)]
    kern = functools.partial(_ffn_kernel, s=s, f=f, split_in=split_in, split_out=split_out,
                             prompt_tiles=prompt_tiles)
    return pl.pallas_call(
        kern,
        grid=(n // tm,),
        in_specs=(list(pair) if split_in else [row]) + [
            _mod_spec(d, group_rows // tm),
            _resident(w_in.shape),
            _resident(w_out.shape),
            _resident(g.shape),
            _resident(b.shape),
        ],
        out_specs=out_specs,
        out_shape=out_shape,
        compiler_params=_params(("arbitrary",)),
        name=f"ffn_s{s}",
    )(*xs, mod, w_in, w_out, g, b)


def _mup_kernel(x_ref, mod_ref, w_ref, xm_ref, z_ref, *, di):
    xmod = _modulate(x_ref[...], mod_ref, 1).astype(BF16)
    r = _dot(xmod, w_ref[...])
    xm_ref[...] = r[:, :di]
    z_ref[...] = r[:, di:].astype(z_ref.dtype)


def _mup_call(x, mod, w_up, *, group_rows):
    tm = TM_UP
    n, d = x.shape
    di = w_up.shape[1] // 2
    out = pl.BlockSpec((tm, di), lambda i: (i, 0))
    return pl.pallas_call(
        functools.partial(_mup_kernel, di=di),
        grid=(n // tm,),
        in_specs=[
            pl.BlockSpec((tm, d), lambda i: (i, 0)),
            _mod_spec(d, group_rows // tm),
            _resident(w_up.shape),
        ],
        out_specs=[out, out],
        out_shape=[jax.ShapeDtypeStruct((n, di), F32), jax.ShapeDtypeStruct((n, di), BF16)],
        compiler_params=_params(("arbitrary",)),
        name="mlstm_up",
    )(x, mod, w_up)


def _mqk_kernel(xm_ref, prev_ref, next_ref, wc_ref, bc_ref, wqk_ref, wg_ref,
                xc_ref, q_ref, k_ref, pre_ref, ext_ref, *, tm, di, conv_w, prompt_tiles,
                prompt_seq, sample_seq):
    i = pl.program_id(0)
    halo = SUBLANES
    seq = jnp.where(i < prompt_tiles, prompt_seq, sample_seq)
    starts_seq = ((i * tm) & (seq - 1)) == 0
    ends_seq = ((i * tm + tm) & (seq - 1)) == 0
    ext_ref[0:halo, :] = jnp.where(starts_seq, 0.0, prev_ref[...])
    ext_ref[halo:halo + tm, :] = xm_ref[...]
    ext_ref[halo + tm:2 * halo + tm, :] = jnp.where(ends_seq, 0.0, next_ref[...])
    acc = jnp.broadcast_to(bc_ref[...], (tm, di))
    for j in range(conv_w):
        off = j - conv_w // 2
        acc = acc + ext_ref[halo + off:halo + off + tm, :] * wc_ref[j:j + 1, :]
    xc = _silu(acc).astype(BF16)
    xc_ref[...] = xc
    qk = _dot(xc, wqk_ref[...]).astype(BF16)
    q_ref[...] = qk[:, :di]
    k_ref[...] = qk[:, di:]
    pre_ref[...] = _dot(qk, wg_ref[...])


def _mqk_call(xm, w_conv, b_conv, w_qk, wg_qk, *, prompt_rows, prompt_seq, sample_seq):
    tm = TM_MIX
    n, di = xm.shape
    assert prompt_seq % tm == 0 and sample_seq % tm == 0 and w_conv.shape[0] // 2 <= SUBLANES
    halo_blocks = n // SUBLANES
    per_tile = tm // SUBLANES
    row = pl.BlockSpec((tm, di), lambda i: (i, 0))
    kern = functools.partial(
        _mqk_kernel, tm=tm, di=di, conv_w=w_conv.shape[0], prompt_tiles=prompt_rows // tm,
        prompt_seq=prompt_seq, sample_seq=sample_seq)
    return pl.pallas_call(
        kern,
        grid=(n // tm,),
        in_specs=[
            row,
            pl.BlockSpec((SUBLANES, di), lambda i: (jnp.maximum(i * per_tile - 1, 0), 0)),
            pl.BlockSpec((SUBLANES, di),
                         lambda i: (jnp.minimum((i + 1) * per_tile, halo_blocks - 1), 0)),
            _resident(w_conv.shape),
            _resident(b_conv.shape),
            _resident(w_qk.shape),
            _resident(wg_qk.shape),
        ],
        out_specs=[row, row, row, pl.BlockSpec((tm, LANES), lambda i: (i, 0))],
        out_shape=[
            jax.ShapeDtypeStruct((n, di), BF16),
            jax.ShapeDtypeStruct((n, di), BF16),
            jax.ShapeDtypeStruct((n, di), BF16),
            jax.ShapeDtypeStruct((n, LANES), F32),
        ],
        scratch_shapes=[pltpu.VMEM((tm + 2 * SUBLANES, di), F32)],
        compiler_params=_params(("arbitrary",)),
        name="mlstm_conv_qk",
    )(xm, xm, xm, w_conv, b_conv, w_qk, wg_qk)


def _mvo_kernel(xm_ref, pre_ref, wv_ref, wo_ref, bo_ref, wg_ref, bg_ref, v_ref, o_ref, preo_ref):
    xm = xm_ref[...].astype(BF16)
    v = _dot(xm, wv_ref[...]).astype(BF16)
    v_ref[...] = v
    o_ref[...] = _sigmoid(_dot(xm, wo_ref[...]) + bo_ref[...]).astype(o_ref.dtype)
    preo_ref[...] = pre_ref[...] + _dot(v, wg_ref[...]) + bg_ref[...]


def _mvo_call(xm, pre_qk, w_v, w_o, b_o, wg_v, bg):
    tm = TM_MIX
    n, di = xm.shape
    row = pl.BlockSpec((tm, di), lambda i: (i, 0))
    gate = pl.BlockSpec((tm, LANES), lambda i: (i, 0))
    return pl.pallas_call(
        _mvo_kernel,
        grid=(n // tm,),
        in_specs=[row, gate, _resident(w_v.shape), _resident(w_o.shape), _resident(b_o.shape),
                  _resident(wg_v.shape), _resident(bg.shape)],
        out_specs=[row, row, gate],
        out_shape=[
            jax.ShapeDtypeStruct((n, di), BF16),
            jax.ShapeDtypeStruct((n, di), BF16),
            jax.ShapeDtypeStruct((n, LANES), F32),
        ],
        compiler_params=_params(("arbitrary",)),
        name="mlstm_v_ogate",
    )(xm, pre_qk, w_v, w_o, b_o, wg_v, bg)


def _mlstm_core_kernel(*refs, nc, dh, chunk, has_state, emit_state):
    refs = list(refs)
    q_ref, k_ref, v_ref, g_ref = refs[:4]
    pos = 4
    if has_state:
        c0_ref, n0_ref, m0_ref = refs[pos:pos + 3]
        pos += 3
    h_ref = refs[pos]
    pos += 1
    if emit_state:
        cn_ref, nn_ref, mn_ref = refs[pos:pos + 3]
        pos += 3
    c_sc, cb_sc, n_sc, m_sc = refs[pos:pos + 4]

    d = pl.program_id(1)
    c = pl.program_id(2)
    L = chunk
    qscale = dh ** -0.5

    @pl.when(c == 0)
    def _():
        if has_state:
            c_sc[...] = c0_ref[0, 0, 0]
            cb_sc[...] = c0_ref[0, 0, 0].astype(BF16)
            n_sc[...] = n0_ref[0, 0, 0]
            m_sc[...] = m0_ref[0]
        else:
            c_sc[...] = jnp.zeros_like(c_sc)
            cb_sc[...] = jnp.zeros_like(cb_sc)
            n_sc[...] = jnp.zeros_like(n_sc)
            m_sc[...] = jnp.zeros_like(m_sc)

    gates = g_ref[...]
    gates_t = gates.T
    fwd = d == 0
    t_idx = lax.broadcasted_iota(jnp.int32, (L, L), 0)
    s_idx = lax.broadcasted_iota(jnp.int32, (L, L), 1)
    sgn = 1 - 2 * d
    causal = (t_idx - s_idx) * sgn >= 0
    causal_t = (s_idx - t_idx) * sgn >= 0

    for h in range(M_HEADS):
        lanes = slice(h * dh, (h + 1) * dh)
        li, lf = h, M_HEADS + h
        bi, bf = 2 * M_HEADS + h, 3 * M_HEADS + h
        i_col = jnp.where(fwd, gates[:, li:li + 1], gates[:, bi:bi + 1])
        f_col = _log_sigmoid(jnp.where(fwd, gates[:, lf:lf + 1], gates[:, bf:bf + 1]))
        i_row = jnp.where(fwd, gates_t[li:li + 1, :], gates_t[bi:bi + 1, :])
        f_row = _log_sigmoid(jnp.where(fwd, gates_t[lf:lf + 1, :], gates_t[bf:bf + 1, :]))
        b_col = jnp.sum(jnp.where(causal, f_row, 0.0), axis=1, keepdims=True)
        b_row = jnp.sum(jnp.where(causal_t, f_col, 0.0), axis=0, keepdims=True)

        q = q_ref[:, lanes]
        k = k_ref[:, lanes]
        v = v_ref[:, lanes]
        m_prev = m_sc[:, h:h + 1]
        c_prev = c_sc[h]
        n_prev = n_sc[h:h + 1, :]
        dmat = jnp.where(causal, b_col - b_row + i_row, -jnp.inf)
        inter = b_col + m_prev
        mr = jnp.maximum(jnp.max(dmat, axis=1, keepdims=True), inter)
        qk = lax.dot_general(q, k, (((1,), (1,)), ((), ())), preferred_element_type=F32)
        p = qk * (qscale * jnp.exp(dmat - mr))
        w_in = qscale * jnp.exp(inter - mr)
        num = _dot(p.astype(BF16), v) + w_in * _dot(q, cb_sc[h])
        qn = jnp.sum(q.astype(F32) * n_prev, axis=1, keepdims=True)
        den = jnp.sum(p, axis=1, keepdims=True) + w_in * qn
        hout = num * (1.0 / jnp.maximum(jnp.abs(den), jnp.exp(-mr)))
        h_ref[0, :, lanes] = hout.astype(h_ref.dtype)

        g = jnp.sum(f_row, axis=1, keepdims=True)
        lw = g - b_col + i_col
        m_new = jnp.maximum(g + m_prev, jnp.max(lw, axis=0, keepdims=True))
        decay = jnp.exp(g + m_prev - m_new)
        kw = k.astype(F32) * jnp.exp(lw - m_new)
        c_new = decay * c_prev + lax.dot_general(
            kw.astype(BF16), v, (((0,), (0,)), ((), ())), preferred_element_type=F32)
        c_sc[h] = c_new
        cb_sc[h] = c_new.astype(BF16)
        n_sc[h:h + 1, :] = decay * n_prev + jnp.sum(kw, axis=0, keepdims=True)
        m_sc[:, h:h + 1] = m_new

    if emit_state:
        @pl.when(c == nc - 1)
        def _():
            cn_ref[0, 0, 0] = c_sc[...]
            nn_ref[0, 0, 0] = n_sc[...]
            mn_ref[0] = m_sc[...]


def _mlstm_core_call(q, k, v, pre, state, *, row0, batch, seq, emit_state):
    di = q.shape[1]
    dh = di // M_HEADS
    chunk = min(MLSTM_CHUNK, seq)
    nc = seq // chunk
    chunk0 = row0 // chunk
    has_state = state is not None

    def chunk_of(b, d, c):
        return b * nc + c + d * (nc - 1 - 2 * c)

    tok = pl.BlockSpec((chunk, di), lambda b, d, c: (chunk0 + chunk_of(b, d, c), 0))
    in_specs = [tok, tok, tok,
                pl.BlockSpec((chunk, LANES), lambda b, d, c: (chunk0 + chunk_of(b, d, c), 0))]
    args = [q, k, v, pre]
    state_c = pl.BlockSpec((1, 1, 1, M_HEADS, dh, dh), lambda b, d, c: (b, 0, d, 0, 0, 0))
    state_n = pl.BlockSpec((1, 1, 1, M_HEADS, dh), lambda b, d, c: (b, 0, d, 0, 0))
    state_m = pl.BlockSpec((1, 1, LANES), lambda b, d, c: (b * 2 + d, 0, 0))
    if has_state:
        in_specs += [state_c, state_n, state_m]
        args += list(state)
    out_specs = [pl.BlockSpec((1, chunk, di), lambda b, d, c: (d, chunk_of(b, d, c), 0))]
    out_shape = [jax.ShapeDtypeStruct((2, batch * seq, di), BF16)]
    if emit_state:
        out_specs += [state_c, state_n, state_m]
        out_shape += [
            jax.ShapeDtypeStruct((batch, 1, 2, M_HEADS, dh, dh), F32),
            jax.ShapeDtypeStruct((batch, 1, 2, M_HEADS, dh), F32),
            jax.ShapeDtypeStruct((batch * 2, 1, LANES), F32),
        ]
    kern = functools.partial(_mlstm_core_kernel, nc=nc, dh=dh, chunk=chunk,
                             has_state=has_state, emit_state=emit_state)
    return pl.pallas_call(
        kern,
        grid=(batch, 2, nc),
        in_specs=in_specs,
        out_specs=out_specs,
        out_shape=out_shape,
        scratch_shapes=[pltpu.VMEM((M_HEADS, dh, dh), F32), pltpu.VMEM((M_HEADS, dh, dh), BF16),
                        pltpu.VMEM((M_HEADS, dh), F32), pltpu.VMEM((1, LANES), F32)],
        compiler_params=_params(("arbitrary",) * 3),
        name="mlstm_core_state" if emit_state else "mlstm_core",
    )(*args)


def _mdown_kernel(x_ref, mod_ref, o_ref, hpf_ref, hpb_ref, hsf_ref, hsb_ref, xc_ref, z_ref,
                  ng_ref, skip_ref, wd_ref, g_ref, b_ref, out_ref, *, prompt_tiles, dh):
    i = pl.program_id(0)
    hsum = jnp.where(i < prompt_tiles, hpf_ref[0].astype(F32) + hpb_ref[0].astype(F32),
                     hsf_ref[0].astype(F32) + hsb_ref[0].astype(F32))
    parts = []
    for h in range(M_HEADS):
        hh = hsum[:, h * dh:(h + 1) * dh]
        mu = jnp.mean(hh, axis=-1, keepdims=True)
        hc = hh - mu
        var = jnp.mean(hc * hc, axis=-1, keepdims=True)
        parts.append(hc * lax.rsqrt(var + LN_EPS))
    hn = jnp.concatenate(parts, axis=-1) * ng_ref[...]
    mixed = ((o_ref[...].astype(F32) * hn + skip_ref[...] * xc_ref[...].astype(F32))
             * _silu(z_ref[...].astype(F32)))
    y = _dot(mixed.astype(BF16), wd_ref[...])
    out_ref[...] = _post_norm(x_ref[...], y, mod_ref, 1, g_ref, b_ref, 1.0)


def _mdown_call(x, mod, o, h_p, h_s, xc, z, norm_g, skip, w_down, g, b, *, group_rows):
    tm = TM_MIX
    n, d = x.shape
    di = o.shape[1]
    prompt_tiles = h_p.shape[1] // tm
    row = pl.BlockSpec((tm, di), lambda i: (i, 0))
    xrow = pl.BlockSpec((tm, d), lambda i: (i, 0))

    def hspec(direction, first_group):
        if first_group:
            return pl.BlockSpec((1, tm, di),
                                lambda i: (direction, jnp.minimum(i, prompt_tiles - 1), 0))
        return pl.BlockSpec((1, tm, di),
                            lambda i: (direction, jnp.maximum(i - prompt_tiles, 0), 0))

    consts = [norm_g, skip, w_down, g, b]
    return pl.pallas_call(
        functools.partial(_mdown_kernel, prompt_tiles=prompt_tiles, dh=di // M_HEADS),
        grid=(n // tm,),
        in_specs=[xrow, _mod_spec(d, group_rows // tm), row,
                  hspec(0, True), hspec(1, True), hspec(0, False), hspec(1, False), row, row]
        + [_resident(a.shape) for a in consts],
        out_specs=xrow,
        out_shape=jax.ShapeDtypeStruct((n, d), F32),
        compiler_params=_params(("arbitrary",)),
        name="mlstm_down",
    )(x, mod, o, h_p, h_p, h_s, h_s, xc, z, *consts)


def _conf_kernel(x_ref, mod_ref, w1_ref, b1_ref, wdw_ref, bdw_ref, lg_ref, lb_ref, w2_ref,
                 b2_ref, g_ref, b_ref, out_ref, sh_ref, conv_ref, *, tm, d, conv_w, seg_len):
    half = conv_w // 2
    gap = 2 * SUBLANES
    assert half < gap
    n_seg = tm // seg_len
    stride = seg_len + gap
    n_rows = gap + n_seg * stride
    conv_rows = 4 * SUBLANES

    x = x_ref[...]
    xm = _modulate(x, mod_ref, 1).astype(BF16)
    ag = _dot(xm, w1_ref[...]) + b1_ref[...]
    glu = ag[:, :d] * _sigmoid(ag[:, d:])
    for s in range(n_seg + 1):
        sh_ref[0, s * stride:s * stride + gap, :] = jnp.zeros((gap, d), F32)
    for s in range(n_seg):
        sh_ref[0, gap + s * stride:gap + s * stride + seg_len, :] = (
            glu[s * seg_len:(s + 1) * seg_len, :])
    for r in range(1, SUBLANES):
        sh_ref[r, 0:n_rows - SUBLANES, :] = sh_ref[0, r:r + n_rows - SUBLANES, :]

    groups = conv_rows // SUBLANES
    for s in range(n_seg):
        for r0 in range(0, seg_len, conv_rows):
            acc = jnp.broadcast_to(bdw_ref[...], (groups, SUBLANES, d))
            for j in range(conv_w):
                r = (j - half) % SUBLANES
                y = gap + s * stride + r0 + (j - half) - r
                tap = sh_ref[r, y:y + conv_rows, :].reshape(groups, SUBLANES, d)
                acc = acc + tap * wdw_ref[j]
            conv_ref[s * seg_len + r0:s * seg_len + r0 + conv_rows, :] = acc.reshape(conv_rows, d)
    hact = _silu(_layer_norm(conv_ref[...], lg_ref[...], lb_ref[...]))
    y = _dot(hact.astype(BF16), w2_ref[...]) + b2_ref[...]
    out_ref[...] = _post_norm(x, y, mod_ref, 1, g_ref, b_ref, 1.0)


def _conf_call(x, mod, w1, b1, wdw, bdw, lg, lb, w2, b2, g, b, *, group_rows, row0, rows,
               seg_len):
    tm = TM_MIX
    _, d = x.shape
    assert tm % seg_len == 0 and seg_len % (4 * SUBLANES) == 0
    assert row0 % tm == 0 and rows % tm == 0
    tile0 = row0 // tm
    gap = 2 * SUBLANES
    n_rows = gap + (tm // seg_len) * (seg_len + gap)
    kern = functools.partial(_conf_kernel, tm=tm, d=d, conv_w=wdw.shape[0], seg_len=seg_len)
    consts = [w1, b1, wdw, bdw, lg, lb, w2, b2, g, b]
    return pl.pallas_call(
        kern,
        grid=(rows // tm,),
        in_specs=[pl.BlockSpec((tm, d), lambda i: (tile0 + i, 0)),
                  pl.BlockSpec((1, N_MOD, d),
                               lambda i: ((tile0 + i) // (group_rows // tm), 0, 0))]
        + [_resident(a.shape) for a in consts],
        out_specs=pl.BlockSpec((tm, d), lambda i: (i, 0)),
        out_shape=jax.ShapeDtypeStruct((rows, d), F32),
        scratch_shapes=[pltpu.VMEM((SUBLANES, n_rows, d), F32), pltpu.VMEM((tm, d), F32)],
        compiler_params=_params(("arbitrary",)),
        name=f"conformer_conv_seg{seg_len}",
    )(x, mod, *consts)


def _row(a):
    return a.reshape(1, -1)


def kernel(x_prompt, x_sample, state_C, state_n, state_m, c, c_ctx, w_mod, b_mod, ln_g, ln_b, ffn_w_in, ffn_w_out, m_w_up, m_w_conv, m_b_conv, m_w_qk, m_w_v, m_w_gate, m_b_gate, m_w_o, m_b_o, m_norm_g, m_skip, m_w_down, cv_w_pw1, cv_b_pw1, cv_w_dw, cv_b_dw, cv_ln_g, cv_ln_b, cv_w_pw2, cv_b_pw2):
    bp, sp, d = x_prompt.shape
    bs, ss, _ = x_sample.shape
    n_p, n_s = bp * sp, bs * ss
    heads = M_HEADS
    group_rows = ss
    assert n_p == group_rows and sp & (sp - 1) == 0 and ss & (ss - 1) == 0
    assert 1 + bs <= COND_ROWS and GRID_W & (GRID_W - 1) == 0
    assert m_w_up.shape[0] == 1 and 4 * heads <= LANES

    cond = jnp.concatenate([c_ctx[None], c, jnp.zeros((COND_ROWS - 1 - bs, d), F32)], axis=0)
    mod_all = _mod_call(cond, w_mod, b_mod).reshape(DEPTH, COND_ROWS, N_MOD, d)

    xs = (x_prompt.reshape(n_p, d), x_sample.reshape(n_s, d))
    new_c = new_n = new_m = None
    for i in range(DEPTH):
        mod = mod_all[i]
        ffn = functools.partial(_ffn_call, prompt_rows=n_p, group_rows=group_rows)
        (x,) = ffn(xs, mod, ffn_w_in[i, 0].astype(BF16), ffn_w_out[i, 0].astype(BF16),
                   _row(ln_g[i, 0]), _row(ln_b[i, 0]), s=0)
        j = i // 2
        if i % 2 == 0:
            di = m_w_v.shape[1]
            xm, z = _mup_call(x, mod, m_w_up[j].astype(BF16), group_rows=group_rows)
            n_gate = m_w_gate.shape[2]
            wg = jnp.pad(m_w_gate[j], ((0, 0), (0, LANES - n_gate))).astype(BF16)
            bg = jnp.pad(m_b_gate[j], (0, LANES - n_gate)).reshape(1, LANES)
            xc, q, k, pre_qk = _mqk_call(
                xm, m_w_conv[j], _row(m_b_conv[j]), m_w_qk[j].astype(BF16), wg[:2 * di],
                prompt_rows=n_p, prompt_seq=sp, sample_seq=ss)
            v, o, pre = _mvo_call(xm, pre_qk, m_w_v[j].astype(BF16), m_w_o[j].astype(BF16),
                                  _row(m_b_o[j]), wg[2 * di:], bg)
            h_p, new_c, new_n, m_p = _mlstm_core_call(
                q, k, v, pre, None, row0=0, batch=bp, seq=sp, emit_state=True)
            m0 = jnp.pad(state_m[:, j].reshape(bs * 2, 1, heads),
                         ((0, 0), (0, 0), (0, LANES - heads)))
            (h_s,) = _mlstm_core_call(
                q, k, v, pre, (state_C[:, j:j + 1], state_n[:, j:j + 1], m0),
                row0=n_p, batch=bs, seq=ss, emit_state=False)
            x = (_mdown_call(x, mod, o, h_p, h_s, xc, z, _row(m_norm_g[j]), _row(m_skip[j]),
                             m_w_down[j].astype(BF16), _row(ln_g[i, 1]), _row(ln_b[i, 1]),
                             group_rows=group_rows),)
            new_m = m_p[:, 0, :heads].reshape(bp, 1, 2, heads)
        else:
            conf = functools.partial(
                _conf_call, x, mod, cv_w_pw1[j].astype(BF16), _row(cv_b_pw1[j]),
                jnp.broadcast_to(cv_w_dw[j][:, None, :], (cv_w_dw.shape[1], SUBLANES, d)),
                _row(cv_b_dw[j]), _row(cv_ln_g[j]), _row(cv_ln_b[j]), cv_w_pw2[j].astype(BF16),
                _row(cv_b_pw2[j]), _row(ln_g[i, 1]), _row(ln_b[i, 1]), group_rows=group_rows)
            x = (conf(row0=0, rows=n_p, seg_len=sp), conf(row0=n_p, rows=n_s, seg_len=GRID_W))
        xs = ffn(x, mod, ffn_w_in[i, 1].astype(BF16), ffn_w_out[i, 1].astype(BF16),
                 _row(ln_g[i, 2]), _row(ln_b[i, 2]), s=2, split_out=(i == DEPTH - 1))
    y_p, y_s = xs
    return (y_p.reshape(bp, sp, d), y_s.reshape(bs, ss, d), new_c, new_n, new_m)
```

```python
import functools

import jax
import jax.numpy as jnp
from jax import lax
from jax.experimental import pallas as pl
from jax.experimental.pallas import tpu as pltpu

F32 = jnp.float32
BF16 = jnp.bfloat16

DEPTH = 2
GRID_W = 64
M_HEADS = 4
N_SUB = 3
N_MOD = 3 * N_SUB
ALPHA = (2 * DEPTH) ** 0.25
LN_EPS = 1e-5

SUBLANES = 8
BF16_SUBLANES = 16
LANES = 128
VMEM_LIMIT = 56 * 1024 * 1024

COND_ROWS = 8

TM_FFN = 512
TM_UP = 512
TM_MIX = 256
CONF_SUBTILES = 2
MLSTM_CHUNK = 256


def _resident(shape):
    nd = len(shape)
    return pl.BlockSpec(shape, lambda *_: (0,) * nd, pipeline_mode=pl.Buffered(1))


def _params(sem):
    return pltpu.CompilerParams(dimension_semantics=sem, vmem_limit_bytes=VMEM_LIMIT)


def _sigmoid(x):
    return 1.0 / (1.0 + jnp.exp(-x))


def _silu(x):
    return x * _sigmoid(x)


def _log_sigmoid(x):
    return jnp.minimum(x, 0.0) - jnp.log1p(jnp.exp(-jnp.abs(x)))


def _layer_norm(x, g, b):
    mu = jnp.mean(x, axis=-1, keepdims=True)
    xc = x - mu
    var = jnp.mean(xc * xc, axis=-1, keepdims=True)
    return xc * lax.rsqrt(var + LN_EPS) * g + b


def _modulate(x, mod_ref, s):
    shift = mod_ref[0, 3 * s:3 * s + 1, :]
    scale = mod_ref[0, 3 * s + 1:3 * s + 2, :]
    return x * (1.0 + scale) + shift


def _post_norm(x, y, mod_ref, s, g_ref, b_ref, res_w):
    gate = mod_ref[0, 3 * s + 2:3 * s + 3, :]
    return _layer_norm(ALPHA * x + (res_w * gate) * y, g_ref[...], b_ref[...])


def _dot(a, b):
    return jnp.dot(a, b, preferred_element_type=F32)


def _mod_spec(d, tiles_per_group):
    return pl.BlockSpec((1, N_MOD, d), lambda i: (i // tiles_per_group, 0, 0))


def _group_specs(tm, width, prompt_tiles):
    return (pl.BlockSpec((tm, width), lambda i: (jnp.minimum(i, prompt_tiles - 1), 0)),
            pl.BlockSpec((tm, width), lambda i: (jnp.maximum(i - prompt_tiles, 0), 0)))


def _cast_jobs(weights, n_steps):
    in_specs, out_specs, out_shape, args, blocks = [], [], [], [], []
    for w, lead, rows in weights:
        r, c = w.shape[-2:]
        assert r % rows == 0 and rows % BF16_SUBLANES == 0 and r // rows <= n_steps
        nb = r // rows
        in_specs.append(pl.BlockSpec(
            (1,) * len(lead) + (rows, c),
            lambda i, lead=lead, nb=nb: (*lead, jnp.minimum(i, nb - 1), 0)))
        out_specs.append(pl.BlockSpec((rows, c), lambda i, nb=nb: (jnp.minimum(i, nb - 1), 0)))
        out_shape.append(jax.ShapeDtypeStruct((r, c), BF16))
        args.append(w)
        blocks.append(nb)
    return in_specs, out_specs, out_shape, args, tuple(blocks)


def _cast_step(i, in_refs, out_refs, blocks):
    for src, dst, nb in zip(in_refs, out_refs, blocks):
        lead = (0,) * (len(src.shape) - 2)

        @pl.when(i < nb)
        def _(src=src, dst=dst, lead=lead):
            dst[...] = src[lead].astype(BF16)


def _mod_kernel(c_ref, w_ref, b_ref, o_ref):
    cond = _silu(c_ref[...])
    o_ref[0] = _dot(cond.astype(BF16), w_ref[0].astype(BF16)) + b_ref[0]


def _mod_call(cond, w_mod, b_mod):
    depth, d, n = w_mod.shape
    tn = n // 8
    return pl.pallas_call(
        _mod_kernel,
        grid=(depth, n // tn),
        in_specs=[
            pl.BlockSpec((COND_ROWS, d), lambda l, j: (0, 0)),
            pl.BlockSpec((1, d, tn), lambda l, j: (l, 0, j)),
            pl.BlockSpec((1, 1, tn), lambda l, j: (l, 0, j)),
        ],
        out_specs=pl.BlockSpec((1, COND_ROWS, tn), lambda l, j: (l, 0, j)),
        out_shape=jax.ShapeDtypeStruct((depth, COND_ROWS, n), F32),
        compiler_params=_params(("arbitrary", "arbitrary")),
        name="adaln_mod",
    )(cond, w_mod, b_mod.reshape(depth, 1, n))


def _ffn_kernel(*refs, s, f, split_in, split_out, prompt_tiles, cast_blocks):
    refs = list(refs)
    n_x = 2 if split_in else 1
    x_refs = refs[:n_x]
    mod_ref, win_ref, wout_ref, g_ref, b_ref = refs[n_x:n_x + 5]
    pos = n_x + 5
    n_cast = len(cast_blocks)
    cast_in = refs[pos:pos + n_cast]
    pos += n_cast
    o_refs = refs[pos:pos + (2 if split_out else 1)]
    pos += len(o_refs)
    i = pl.program_id(0)
    _cast_step(i, cast_in, refs[pos:pos + n_cast], cast_blocks)

    if split_in:
        x = jnp.where(i < prompt_tiles, x_refs[0][...], x_refs[1][...])
    else:
        x = x_refs[0][...]
    xm = _modulate(x, mod_ref, s).astype(BF16)
    au = _dot(xm, win_ref[...])
    hmid = (_silu(au[:, :f]) * au[:, f:]).astype(BF16)
    y = _dot(hmid, wout_ref[...])
    res = _post_norm(x, y, mod_ref, s, g_ref, b_ref, 0.5)
    if split_out:
        @pl.when(i < prompt_tiles)
        def _():
            o_refs[0][...] = res

        @pl.when(i >= prompt_tiles)
        def _():
            o_refs[1][...] = res
    else:
        o_refs[0][...] = res


def _ffn_call(xs, mod, w_in, w_out, g, b, *, s, prompt_rows, group_rows, split_out=False,
              next_w=None):
    tm = TM_FFN
    split_in = len(xs) == 2
    n = sum(x.shape[0] for x in xs)
    d = xs[0].shape[1]
    f = w_out.shape[0]
    n_tiles = n // tm
    prompt_tiles = prompt_rows // tm
    row = pl.BlockSpec((tm, d), lambda i: (i, 0))
    pair = _group_specs(tm, d, prompt_tiles)
    in_specs = (list(pair) if split_in else [row]) + [
        _mod_spec(d, group_rows // tm),
        _resident(w_in.shape),
        _resident(w_out.shape),
        _resident(g.shape),
        _resident(b.shape),
    ]
    args = list(xs) + [mod, w_in, w_out, g, b]
    if split_out:
        out_specs = list(pair)
        out_shape = [jax.ShapeDtypeStruct((prompt_rows, d), F32),
                     jax.ShapeDtypeStruct((n - prompt_rows, d), F32)]
    else:
        out_specs = [row]
        out_shape = [jax.ShapeDtypeStruct((n, d), F32)]
    cast_blocks = ()
    if next_w is not None:
        nw_in, nw_out, layer, posn = next_w
        c_in, c_out, c_shape, c_args, cast_blocks = _cast_jobs(
            [(nw_in, (layer, posn), 2 * BF16_SUBLANES),
             (nw_out, (layer, posn), 8 * BF16_SUBLANES)], n_tiles)
        in_specs += c_in
        out_specs += c_out
        out_shape += c_shape
        args += c_args
    kern = functools.partial(_ffn_kernel, s=s, f=f, split_in=split_in, split_out=split_out,
                             prompt_tiles=prompt_tiles, cast_blocks=cast_blocks)
    return pl.pallas_call(
        kern,
        grid=(n_tiles,),
        in_specs=in_specs,
        out_specs=out_specs,
        out_shape=out_shape,
        compiler_params=_params(("arbitrary",)),
        name=f"ffn_s{s}",
    )(*args)


def _mup_kernel(*refs, di, cast_blocks):
    x_ref, mod_ref, w_ref = refs[:3]
    n_cast = len(cast_blocks)
    xm_ref, z_ref = refs[3 + n_cast:5 + n_cast]
    _cast_step(pl.program_id(0), refs[3:3 + n_cast], refs[5 + n_cast:], cast_blocks)
    xmod = _modulate(x_ref[...], mod_ref, 1).astype(BF16)
    r = _dot(xmod, w_ref[...])
    xm_ref[...] = r[:, :di]
    z_ref[...] = r[:, di:].astype(z_ref.dtype)


def _mup_call(x, mod, w_up, later_weights, *, group_rows):
    tm = TM_UP
    n, d = x.shape
    di = w_up.shape[1] // 2
    out = pl.BlockSpec((tm, di), lambda i: (i, 0))
    c_in, c_out, c_shape, c_args, cast_blocks = _cast_jobs(later_weights, n // tm)
    return pl.pallas_call(
        functools.partial(_mup_kernel, di=di, cast_blocks=cast_blocks),
        grid=(n // tm,),
        in_specs=[
            pl.BlockSpec((tm, d), lambda i: (i, 0)),
            _mod_spec(d, group_rows // tm),
            _resident(w_up.shape),
        ] + c_in,
        out_specs=[out, out] + c_out,
        out_shape=[jax.ShapeDtypeStruct((n, di), F32), jax.ShapeDtypeStruct((n, di), BF16)]
        + c_shape,
        compiler_params=_params(("arbitrary",)),
        name="mlstm_up",
    )(x, mod, w_up, *c_args)


def _mqk_kernel(xm_ref, prev_ref, next_ref, wc_ref, bc_ref, wqk_ref, wg_ref,
                xc_ref, q_ref, k_ref, pre_ref, ext_ref, *, tm, di, conv_w, prompt_tiles,
                prompt_seq, sample_seq):
    i = pl.program_id(0)
    halo = SUBLANES
    seq = jnp.where(i < prompt_tiles, prompt_seq, sample_seq)
    starts_seq = ((i * tm) & (seq - 1)) == 0
    ends_seq = ((i * tm + tm) & (seq - 1)) == 0
    ext_ref[0:halo, :] = jnp.where(starts_seq, 0.0, prev_ref[...])
    ext_ref[halo:halo + tm, :] = xm_ref[...]
    ext_ref[halo + tm:2 * halo + tm, :] = jnp.where(ends_seq, 0.0, next_ref[...])
    acc = jnp.broadcast_to(bc_ref[...], (tm, di))
    for j in range(conv_w):
        off = j - conv_w // 2
        acc = acc + ext_ref[halo + off:halo + off + tm, :] * wc_ref[j:j + 1, :]
    xc = _silu(acc).astype(BF16)
    xc_ref[...] = xc
    qk = _dot(xc, wqk_ref[...]).astype(BF16)
    q_ref[...] = qk[:, :di]
    k_ref[...] = qk[:, di:]
    pre_ref[...] = _dot(qk, wg_ref[...])


def _mqk_call(xm, w_conv, b_conv, w_qk, wg_qk, *, prompt_rows, prompt_seq, sample_seq):
    tm = TM_MIX
    n, di = xm.shape
    assert prompt_seq % tm == 0 and sample_seq % tm == 0 and w_conv.shape[0] // 2 <= SUBLANES
    halo_blocks = n // SUBLANES
    per_tile = tm // SUBLANES
    row = pl.BlockSpec((tm, di), lambda i: (i, 0))
    kern = functools.partial(
        _mqk_kernel, tm=tm, di=di, conv_w=w_conv.shape[0], prompt_tiles=prompt_rows // tm,
        prompt_seq=prompt_seq, sample_seq=sample_seq)
    return pl.pallas_call(
        kern,
        grid=(n // tm,),
        in_specs=[
            row,
            pl.BlockSpec((SUBLANES, di), lambda i: (jnp.maximum(i * per_tile - 1, 0), 0)),
            pl.BlockSpec((SUBLANES, di),
                         lambda i: (jnp.minimum((i + 1) * per_tile, halo_blocks - 1), 0)),
            _resident(w_conv.shape),
            _resident(b_conv.shape),
            _resident(w_qk.shape),
            _resident(wg_qk.shape),
        ],
        out_specs=[row, row, row, pl.BlockSpec((tm, LANES), lambda i: (i, 0))],
        out_shape=[
            jax.ShapeDtypeStruct((n, di), BF16),
            jax.ShapeDtypeStruct((n, di), BF16),
            jax.ShapeDtypeStruct((n, di), BF16),
            jax.ShapeDtypeStruct((n, LANES), F32),
        ],
        scratch_shapes=[pltpu.VMEM((tm + 2 * SUBLANES, di), F32)],
        compiler_params=_params(("arbitrary",)),
        name="mlstm_conv_qk",
    )(xm, xm, xm, w_conv, b_conv, w_qk, wg_qk)


def _mvo_kernel(xm_ref, pre_ref, wv_ref, wo_ref, bo_ref, wg_ref, bg_ref, v_ref, o_ref, preo_ref):
    xm = xm_ref[...].astype(BF16)
    v = _dot(xm, wv_ref[...]).astype(BF16)
    v_ref[...] = v
    o_ref[...] = _sigmoid(_dot(xm, wo_ref[...]) + bo_ref[...]).astype(o_ref.dtype)
    preo_ref[...] = pre_ref[...] + _dot(v, wg_ref[...]) + bg_ref[...]


def _mvo_call(xm, pre_qk, w_v, w_o, b_o, wg_v, bg):
    tm = TM_MIX
    n, di = xm.shape
    row = pl.BlockSpec((tm, di), lambda i: (i, 0))
    gate = pl.BlockSpec((tm, LANES), lambda i: (i, 0))
    return pl.pallas_call(
        _mvo_kernel,
        grid=(n // tm,),
        in_specs=[row, gate, _resident(w_v.shape), _resident(w_o.shape), _resident(b_o.shape),
                  _resident(wg_v.shape), _resident(bg.shape)],
        out_specs=[row, row, gate],
        out_shape=[
            jax.ShapeDtypeStruct((n, di), BF16),
            jax.ShapeDtypeStruct((n, di), BF16),
            jax.ShapeDtypeStruct((n, LANES), F32),
        ],
        compiler_params=_params(("arbitrary",)),
        name="mlstm_v_ogate",
    )(xm, pre_qk, w_v, w_o, b_o, wg_v, bg)


def _mlstm_core_kernel(*refs, nc, dh, chunk, has_state, emit_state):
    refs = list(refs)
    q_ref, k_ref, v_ref, g_ref = refs[:4]
    pos = 4
    if has_state:
        c0_ref, n0_ref, m0_ref = refs[pos:pos + 3]
        pos += 3
    h_ref = refs[pos]
    pos += 1
    if emit_state:
        cn_ref, nn_ref, mn_ref = refs[pos:pos + 3]
        pos += 3
    c_sc, cb_sc, n_sc, m_sc = refs[pos:pos + 4]

    d = pl.program_id(1)
    c = pl.program_id(2)
    L = chunk
    qscale = dh ** -0.5
    fresh = nc == 1 and not has_state

    if not fresh:
        @pl.when(c == 0)
        def _():
            if has_state:
                c_sc[...] = c0_ref[0, 0, 0]
                cb_sc[...] = c0_ref[0, 0, 0].astype(BF16)
                n_sc[...] = n0_ref[0, 0, 0]
                m_sc[...] = m0_ref[0]
            else:
                c_sc[...] = jnp.zeros_like(c_sc)
                cb_sc[...] = jnp.zeros_like(cb_sc)
                n_sc[...] = jnp.zeros_like(n_sc)
                m_sc[...] = jnp.zeros_like(m_sc)
    elif emit_state:
        mn_ref[0] = jnp.zeros(mn_ref.shape[1:], F32)

    gates = g_ref[...]
    gates_t = gates.T
    fwd = d == 0
    t_idx = lax.broadcasted_iota(jnp.int32, (L, L), 0)
    s_idx = lax.broadcasted_iota(jnp.int32, (L, L), 1)
    sgn = 1 - 2 * d
    causal = (t_idx - s_idx) * sgn >= 0
    causal_t = (s_idx - t_idx) * sgn >= 0

    for h in range(M_HEADS):
        lanes = slice(h * dh, (h + 1) * dh)
        li, lf = h, M_HEADS + h
        bi, bf = 2 * M_HEADS + h, 3 * M_HEADS + h
        i_col = jnp.where(fwd, gates[:, li:li + 1], gates[:, bi:bi + 1])
        f_col = _log_sigmoid(jnp.where(fwd, gates[:, lf:lf + 1], gates[:, bf:bf + 1]))
        i_row = jnp.where(fwd, gates_t[li:li + 1, :], gates_t[bi:bi + 1, :])
        f_row = _log_sigmoid(jnp.where(fwd, gates_t[lf:lf + 1, :], gates_t[bf:bf + 1, :]))
        b_col = jnp.sum(jnp.where(causal, f_row, 0.0), axis=1, keepdims=True)
        b_row = jnp.sum(jnp.where(causal_t, f_col, 0.0), axis=0, keepdims=True)

        q = q_ref[:, lanes]
        k = k_ref[:, lanes]
        v = v_ref[:, lanes]
        m_prev = 0.0 if fresh else m_sc[:, h:h + 1]
        c_prev = None if fresh else c_sc[h]
        n_prev = None if fresh else n_sc[h:h + 1, :]
        dmat = jnp.where(causal, b_col - b_row + i_row, -jnp.inf)
        inter = b_col + m_prev
        mr = jnp.maximum(jnp.max(dmat, axis=1, keepdims=True), inter)
        qk = lax.dot_general(q, k, (((1,), (1,)), ((), ())), preferred_element_type=F32)
        p = qk * (qscale * jnp.exp(dmat - mr))
        num = _dot(p.astype(BF16), v)
        den = jnp.sum(p, axis=1, keepdims=True)
        if not fresh:
            w_in = qscale * jnp.exp(inter - mr)
            num = num + w_in * _dot(q, cb_sc[h])
            qn = jnp.sum(q.astype(F32) * n_prev, axis=1, keepdims=True)
            den = den + w_in * qn
        hout = num * (1.0 / jnp.maximum(jnp.abs(den), jnp.exp(-mr)))
        h_ref[0, :, lanes] = hout.astype(h_ref.dtype)

        g = jnp.sum(f_row, axis=1, keepdims=True)
        lw = g - b_col + i_col
        m_new = jnp.maximum(g + m_prev, jnp.max(lw, axis=0, keepdims=True))
        kw = k.astype(F32) * jnp.exp(lw - m_new)
        c_new = lax.dot_general(kw.astype(BF16), v, (((0,), (0,)), ((), ())),
                                preferred_element_type=F32)
        n_new = jnp.sum(kw, axis=0, keepdims=True)
        if fresh:
            if emit_state:
                cn_ref[0, 0, 0, h] = c_new
                nn_ref[0, 0, 0, h:h + 1, :] = n_new
                mn_ref[0, :, h:h + 1] = m_new
        else:
            decay = jnp.exp(g + m_prev - m_new)
            c_new = decay * c_prev + c_new
            c_sc[h] = c_new
            cb_sc[h] = c_new.astype(BF16)
            n_sc[h:h + 1, :] = decay * n_prev + n_new
            m_sc[:, h:h + 1] = m_new

    if emit_state and not fresh:
        @pl.when(c == nc - 1)
        def _():
            cn_ref[0, 0, 0] = c_sc[...]
            nn_ref[0, 0, 0] = n_sc[...]
            mn_ref[0] = m_sc[...]


def _mlstm_core_call(q, k, v, pre, state, *, row0, batch, seq, emit_state):
    di = q.shape[1]
    dh = di // M_HEADS
    chunk = min(MLSTM_CHUNK, seq)
    nc = seq // chunk
    chunk0 = row0 // chunk
    has_state = state is not None

    def chunk_of(b, d, c):
        return b * nc + c + d * (nc - 1 - 2 * c)

    tok = pl.BlockSpec((chunk, di), lambda b, d, c: (chunk0 + chunk_of(b, d, c), 0))
    in_specs = [tok, tok, tok,
                pl.BlockSpec((chunk, LANES), lambda b, d, c: (chunk0 + chunk_of(b, d, c), 0))]
    args = [q, k, v, pre]
    state_c = pl.BlockSpec((1, 1, 1, M_HEADS, dh, dh), lambda b, d, c: (b, 0, d, 0, 0, 0))
    state_n = pl.BlockSpec((1, 1, 1, M_HEADS, dh), lambda b, d, c: (b, 0, d, 0, 0))
    state_m = pl.BlockSpec((1, 1, LANES), lambda b, d, c: (b * 2 + d, 0, 0))
    if has_state:
        in_specs += [state_c, state_n, state_m]
        args += list(state)
    out_specs = [pl.BlockSpec((1, chunk, di), lambda b, d, c: (d, chunk_of(b, d, c), 0))]
    out_shape = [jax.ShapeDtypeStruct((2, batch * seq, di), BF16)]
    if emit_state:
        out_specs += [state_c, state_n, state_m]
        out_shape += [
            jax.ShapeDtypeStruct((batch, 1, 2, M_HEADS, dh, dh), F32),
            jax.ShapeDtypeStruct((batch, 1, 2, M_HEADS, dh), F32),
            jax.ShapeDtypeStruct((batch * 2, 1, LANES), F32),
        ]
    kern = functools.partial(_mlstm_core_kernel, nc=nc, dh=dh, chunk=chunk,
                             has_state=has_state, emit_state=emit_state)
    return pl.pallas_call(
        kern,
        grid=(batch, 2, nc),
        in_specs=in_specs,
        out_specs=out_specs,
        out_shape=out_shape,
        scratch_shapes=[pltpu.VMEM((M_HEADS, dh, dh), F32), pltpu.VMEM((M_HEADS, dh, dh), BF16),
                        pltpu.VMEM((M_HEADS, dh), F32), pltpu.VMEM((1, LANES), F32)],
        compiler_params=_params(("arbitrary",) * 3),
        name="mlstm_core_state" if emit_state else "mlstm_core",
    )(*args)


def _mdown_kernel(x_ref, mod_ref, o_ref, hpf_ref, hpb_ref, hsf_ref, hsb_ref, xc_ref, z_ref,
                  ng_ref, skip_ref, wd_ref, g_ref, b_ref, out_ref, *, prompt_tiles, dh):
    i = pl.program_id(0)
    is_prompt = i < prompt_tiles
    hsum = (jnp.where(is_prompt, hpf_ref[0], hsf_ref[0]).astype(F32)
            + jnp.where(is_prompt, hpb_ref[0], hsb_ref[0]).astype(F32))
    parts = []
    for h in range(M_HEADS):
        hh = hsum[:, h * dh:(h + 1) * dh]
        mu = jnp.mean(hh, axis=-1, keepdims=True)
        hc = hh - mu
        var = jnp.mean(hc * hc, axis=-1, keepdims=True)
        parts.append(hc * lax.rsqrt(var + LN_EPS))
    hn = jnp.concatenate(parts, axis=-1) * ng_ref[...]
    mixed = ((o_ref[...].astype(F32) * hn + skip_ref[...] * xc_ref[...].astype(F32))
             * _silu(z_ref[...].astype(F32)))
    y = _dot(mixed.astype(BF16), wd_ref[...])
    out_ref[...] = _post_norm(x_ref[...], y, mod_ref, 1, g_ref, b_ref, 1.0)


def _mdown_call(x, mod, o, h_p, h_s, xc, z, norm_g, skip, w_down, g, b, *, group_rows):
    tm = TM_MIX
    n, d = x.shape
    di = o.shape[1]
    prompt_tiles = h_p.shape[1] // tm
    row = pl.BlockSpec((tm, di), lambda i: (i, 0))
    xrow = pl.BlockSpec((tm, d), lambda i: (i, 0))

    def hspec(direction, first_group):
        if first_group:
            return pl.BlockSpec((1, tm, di),
                                lambda i: (direction, jnp.minimum(i, prompt_tiles - 1), 0))
        return pl.BlockSpec((1, tm, di),
                            lambda i: (direction, jnp.maximum(i - prompt_tiles, 0), 0))

    consts = [norm_g, skip, w_down, g, b]
    return pl.pallas_call(
        functools.partial(_mdown_kernel, prompt_tiles=prompt_tiles, dh=di // M_HEADS),
        grid=(n // tm,),
        in_specs=[xrow, _mod_spec(d, group_rows // tm), row,
                  hspec(0, True), hspec(1, True), hspec(0, False), hspec(1, False), row, row]
        + [_resident(a.shape) for a in consts],
        out_specs=xrow,
        out_shape=jax.ShapeDtypeStruct((n, d), F32),
        compiler_params=_params(("arbitrary",)),
        name="mlstm_down",
    )(x, mod, o, h_p, h_p, h_s, h_s, xc, z, *consts)


def _conf_kernel(x_ref, mod_ref, w1_ref, b1_ref, wdw_ref, bdw_ref, lg_ref, lb_ref, w2_ref,
                 b2_ref, g_ref, b_ref, out_ref, sh_ref, conv_ref, *, tm, d, conv_w, seg_len):
    half = conv_w // 2
    gap = 2 * SUBLANES
    assert half < gap
    n_seg = tm // seg_len
    stride = seg_len + gap
    n_rows = gap + n_seg * stride
    conv_rows = 4 * SUBLANES

    groups = conv_rows // SUBLANES
    for t in range(x_ref.shape[0] // tm):
        rows = slice(t * tm, (t + 1) * tm)
        x = x_ref[rows, :]
        xm = _modulate(x, mod_ref, 1).astype(BF16)
        ag = _dot(xm, w1_ref[...]) + b1_ref[...]
        glu = ag[:, :d] * _sigmoid(ag[:, d:])
        for s in range(n_seg + 1):
            sh_ref[t, 0, s * stride:s * stride + gap, :] = jnp.zeros((gap, d), F32)
        for s in range(n_seg):
            sh_ref[t, 0, gap + s * stride:gap + s * stride + seg_len, :] = (
                glu[s * seg_len:(s + 1) * seg_len, :])
        for r in range(1, SUBLANES):
            sh_ref[t, r, 0:n_rows - SUBLANES, :] = sh_ref[t, 0, r:r + n_rows - SUBLANES, :]

        for s in range(n_seg):
            for r0 in range(0, seg_len, conv_rows):
                acc = jnp.broadcast_to(bdw_ref[...], (groups, SUBLANES, d))
                for j in range(conv_w):
                    r = (j - half) % SUBLANES
                    y = gap + s * stride + r0 + (j - half) - r
                    tap = sh_ref[t, r, y:y + conv_rows, :].reshape(groups, SUBLANES, d)
                    acc = acc + tap * wdw_ref[j]
                c0 = s * seg_len + r0
                conv_ref[t, c0:c0 + conv_rows, :] = acc.reshape(conv_rows, d)
        hact = _silu(_layer_norm(conv_ref[t], lg_ref[...], lb_ref[...]))
        y = _dot(hact.astype(BF16), w2_ref[...]) + b2_ref[...]
        out_ref[rows, :] = _post_norm(x, y, mod_ref, 1, g_ref, b_ref, 1.0)


def _conf_call(x, mod, w1, b1, wdw, bdw, lg, lb, w2, b2, g, b, *, group_rows, row0, rows,
               seg_len):
    tm = TM_MIX
    blk = CONF_SUBTILES * tm
    _, d = x.shape
    assert tm % seg_len == 0 and seg_len % (4 * SUBLANES) == 0
    assert row0 % blk == 0 and rows % blk == 0 and group_rows % blk == 0
    tile0 = row0 // blk
    gap = 2 * SUBLANES
    n_rows = gap + (tm // seg_len) * (seg_len + gap)
    kern = functools.partial(_conf_kernel, tm=tm, d=d, conv_w=wdw.shape[0], seg_len=seg_len)
    consts = [w1, b1, wdw, bdw, lg, lb, w2, b2, g, b]
    return pl.pallas_call(
        kern,
        grid=(rows // blk,),
        in_specs=[pl.BlockSpec((blk, d), lambda i: (tile0 + i, 0)),
                  pl.BlockSpec((1, N_MOD, d),
                               lambda i: ((tile0 + i) // (group_rows // blk), 0, 0))]
        + [_resident(a.shape) for a in consts],
        out_specs=pl.BlockSpec((blk, d), lambda i: (i, 0)),
        out_shape=jax.ShapeDtypeStruct((rows, d), F32),
        scratch_shapes=[pltpu.VMEM((CONF_SUBTILES, SUBLANES, n_rows, d), F32),
                        pltpu.VMEM((CONF_SUBTILES, tm, d), F32)],
        compiler_params=_params(("arbitrary",)),
        name=f"conformer_conv_seg{seg_len}",
    )(x, mod, *consts)


def _row(a):
    return a.reshape(1, -1)


def kernel(x_prompt, x_sample, state_C, state_n, state_m, c, c_ctx, w_mod, b_mod, ln_g, ln_b, ffn_w_in, ffn_w_out, m_w_up, m_w_conv, m_b_conv, m_w_qk, m_w_v, m_w_gate, m_b_gate, m_w_o, m_b_o, m_norm_g, m_skip, m_w_down, cv_w_pw1, cv_b_pw1, cv_w_dw, cv_b_dw, cv_ln_g, cv_ln_b, cv_w_pw2, cv_b_pw2):
    bp, sp, d = x_prompt.shape
    bs, ss, _ = x_sample.shape
    n_p, n_s = bp * sp, bs * ss
    heads = M_HEADS
    group_rows = ss
    assert n_p == group_rows and sp & (sp - 1) == 0 and ss & (ss - 1) == 0
    assert 1 + bs <= COND_ROWS and GRID_W & (GRID_W - 1) == 0
    assert m_w_up.shape[0] == 1 and 4 * heads <= LANES

    cond = jnp.concatenate([c_ctx[None], c, jnp.zeros((COND_ROWS - 1 - bs, d), F32)], axis=0)
    mod_all = _mod_call(cond, w_mod, b_mod).reshape(DEPTH, COND_ROWS, N_MOD, d)

    xs = (x_prompt.reshape(n_p, d), x_sample.reshape(n_s, d))
    new_c = new_n = new_m = None
    ffn_w = (ffn_w_in[0, 0].astype(BF16), ffn_w_out[0, 0].astype(BF16))
    for i in range(DEPTH):
        mod = mod_all[i]
        ffn = functools.partial(_ffn_call, prompt_rows=n_p, group_rows=group_rows)
        x, *ffn_w = ffn(xs, mod, *ffn_w, _row(ln_g[i, 0]), _row(ln_b[i, 0]), s=0,
                        next_w=(ffn_w_in, ffn_w_out, i, 1))
        j = i // 2
        if i % 2 == 0:
            di = m_w_v.shape[1]
            cast_rows = 4 * BF16_SUBLANES
            xm, z, w_qk, w_v, w_o, w_down = _mup_call(
                x, mod, m_w_up[j].astype(BF16),
                [(w, (j,), cast_rows) for w in (m_w_qk, m_w_v, m_w_o, m_w_down)],
                group_rows=group_rows)
            n_gate = m_w_gate.shape[2]
            wg = jnp.pad(m_w_gate[j], ((0, 0), (0, LANES - n_gate))).astype(BF16)
            bg = jnp.pad(m_b_gate[j], (0, LANES - n_gate)).reshape(1, LANES)
            xc, q, k, pre_qk = _mqk_call(
                xm, m_w_conv[j], _row(m_b_conv[j]), w_qk, wg[:2 * di],
                prompt_rows=n_p, prompt_seq=sp, sample_seq=ss)
            v, o, pre = _mvo_call(xm, pre_qk, w_v, w_o, _row(m_b_o[j]), wg[2 * di:], bg)
            h_p, new_c, new_n, m_p = _mlstm_core_call(
                q, k, v, pre, None, row0=0, batch=bp, seq=sp, emit_state=True)
            m0 = jnp.pad(state_m[:, j].reshape(bs * 2, 1, heads),
                         ((0, 0), (0, 0), (0, LANES - heads)))
            (h_s,) = _mlstm_core_call(
                q, k, v, pre, (state_C[:, j:j + 1], state_n[:, j:j + 1], m0),
                row0=n_p, batch=bs, seq=ss, emit_state=False)
            x = (_mdown_call(x, mod, o, h_p, h_s, xc, z, _row(m_norm_g[j]), _row(m_skip[j]),
                             w_down, _row(ln_g[i, 1]), _row(ln_b[i, 1]),
                             group_rows=group_rows),)
            new_m = m_p[:, 0, :heads].reshape(bp, 1, 2, heads)
        else:
            conf = functools.partial(
                _conf_call, x, mod, cv_w_pw1[j].astype(BF16), _row(cv_b_pw1[j]),
                jnp.broadcast_to(cv_w_dw[j][:, None, :], (cv_w_dw.shape[1], SUBLANES, d)),
                _row(cv_b_dw[j]), _row(cv_ln_g[j]), _row(cv_ln_b[j]), cv_w_pw2[j].astype(BF16),
                _row(cv_b_pw2[j]), _row(ln_g[i, 1]), _row(ln_b[i, 1]), group_rows=group_rows)
            x = (conf(row0=0, rows=n_p, seg_len=sp), conf(row0=n_p, rows=n_s, seg_len=GRID_W))
        last = i == DEPTH - 1
        outs = ffn(x, mod, *ffn_w, _row(ln_g[i, 2]), _row(ln_b[i, 2]), s=2, split_out=last,
                   next_w=None if last else (ffn_w_in, ffn_w_out, i + 1, 0))
        xs, ffn_w = (outs, None) if last else (outs[:1], outs[1:])
    y_p, y_s = xs
    return (y_p.reshape(bp, sp, d), y_s.reshape(bs, ss, d), new_c, new_n, new_m)
```

```python
import functools
import math

import jax
import jax.numpy as jnp
from jax import lax
from jax.experimental import pallas as pl
from jax.experimental.pallas import tpu as pltpu

F32 = jnp.float32
BF16 = jnp.bfloat16

DEPTH = 2
GRID_W = 64
M_HEADS = 4
N_SUB = 3
N_MOD = 3 * N_SUB
ALPHA = (2 * DEPTH) ** 0.25
LN_EPS = 1e-5

SUBLANES = 8
BF16_SUBLANES = 16
LANES = 128
VMEM_LIMIT = 56 * 1024 * 1024

COND_ROWS = 8

TM_FFN = 512
TM_UP = 512
TM_MIX = 256
CONF_SUBTILES = 2
FFN_SUBTILES = 2
MDOWN_SUBTILES = 2
MVO_SUBTILES = 2
MLSTM_CHUNK = 256


def _resident(shape):
    nd = len(shape)
    return pl.BlockSpec(shape, lambda *_: (0,) * nd, pipeline_mode=pl.Buffered(1))


def _params(sem):
    return pltpu.CompilerParams(dimension_semantics=sem, vmem_limit_bytes=VMEM_LIMIT)


def _sigmoid(x):
    return 1.0 / (1.0 + jnp.exp(-x))


def _silu(x):
    return x * _sigmoid(x)


def _log_sigmoid(x):
    return jnp.minimum(x, 0.0) - jnp.log1p(jnp.exp(-jnp.abs(x)))


def _layer_norm(x, g, b):
    mu = jnp.mean(x, axis=-1, keepdims=True)
    xc = x - mu
    var = jnp.mean(xc * xc, axis=-1, keepdims=True)
    return xc * lax.rsqrt(var + LN_EPS) * g + b


def _modulate(x, mod_ref, s):
    shift = mod_ref[0, 3 * s:3 * s + 1, :]
    scale = mod_ref[0, 3 * s + 1:3 * s + 2, :]
    return x * (1.0 + scale) + shift


def _post_norm(x, y, mod_ref, s, g_ref, b_ref, res_w):
    gate = mod_ref[0, 3 * s + 2:3 * s + 3, :]
    return _layer_norm(ALPHA * x + (res_w * gate) * y, g_ref[...], b_ref[...])


def _dot(a, b):
    return jnp.dot(a, b, preferred_element_type=F32)


def _mod_spec(d, tiles_per_group):
    return pl.BlockSpec((1, N_MOD, d), lambda i: (i // tiles_per_group, 0, 0))


def _group_specs(tm, width, prompt_tiles):
    return (pl.BlockSpec((tm, width), lambda i: (jnp.minimum(i, prompt_tiles - 1), 0)),
            pl.BlockSpec((tm, width), lambda i: (jnp.maximum(i - prompt_tiles, 0), 0)))


def _cast_jobs(weights, n_steps):
    in_specs, out_specs, out_shape, args, blocks = [], [], [], [], []
    for w, lead, rows in weights:
        r, c = w.shape[-2:]
        assert r % rows == 0 and rows % BF16_SUBLANES == 0 and r // rows <= n_steps
        nb = r // rows
        in_specs.append(pl.BlockSpec(
            (1,) * len(lead) + (rows, c),
            lambda i, lead=lead, nb=nb: (*lead, jnp.minimum(i, nb - 1), 0)))
        out_specs.append(pl.BlockSpec((rows, c), lambda i, nb=nb: (jnp.minimum(i, nb - 1), 0)))
        out_shape.append(jax.ShapeDtypeStruct((r, c), BF16))
        args.append(w)
        blocks.append(nb)
    return in_specs, out_specs, out_shape, args, tuple(blocks)


def _cast_step(i, in_refs, out_refs, blocks):
    for src, dst, nb in zip(in_refs, out_refs, blocks):
        lead = (0,) * (len(src.shape) - 2)

        @pl.when(i < nb)
        def _(src=src, dst=dst, lead=lead):
            dst[...] = src[lead].astype(BF16)


def _mod_kernel(c_ref, w_ref, b_ref, o_ref):
    cond = _silu(c_ref[...])
    o_ref[0] = _dot(cond.astype(BF16), w_ref[0].astype(BF16)) + b_ref[0]


def _mod_call(cond, w_mod, b_mod):
    depth, d, n = w_mod.shape
    tn = n // 8
    return pl.pallas_call(
        _mod_kernel,
        grid=(depth, n // tn),
        in_specs=[
            pl.BlockSpec((COND_ROWS, d), lambda l, j: (0, 0)),
            pl.BlockSpec((1, d, tn), lambda l, j: (l, 0, j)),
            pl.BlockSpec((1, 1, tn), lambda l, j: (l, 0, j)),
        ],
        out_specs=pl.BlockSpec((1, COND_ROWS, tn), lambda l, j: (l, 0, j)),
        out_shape=jax.ShapeDtypeStruct((depth, COND_ROWS, n), F32),
        compiler_params=_params(("arbitrary", "arbitrary")),
        name="adaln_mod",
    )(cond, w_mod, b_mod.reshape(depth, 1, n))


def _ffn_kernel(*refs, s, f, split_in, split_out, prompt_tiles, cast_blocks):
    refs = list(refs)
    n_x = 2 if split_in else 1
    x_refs = refs[:n_x]
    mod_ref, win_ref, wout_ref, g_ref, b_ref = refs[n_x:n_x + 5]
    pos = n_x + 5
    n_cast = len(cast_blocks)
    cast_in = refs[pos:pos + n_cast]
    pos += n_cast
    o_refs = refs[pos:pos + (2 if split_out else 1)]
    pos += len(o_refs)
    i = pl.program_id(0)
    _cast_step(i, cast_in, refs[pos:pos + n_cast], cast_blocks)

    sub = x_refs[0].shape[0] // FFN_SUBTILES
    st = [dict(rows=slice(t * sub, (t + 1) * sub)) for t in range(FFN_SUBTILES)]

    def expand(c):
        rows = c["rows"]
        if split_in:
            c["x"] = jnp.where(i < prompt_tiles, x_refs[0][rows, :], x_refs[1][rows, :])
        else:
            c["x"] = x_refs[0][rows, :]
        c["au"] = _dot(_modulate(c["x"], mod_ref, s).astype(BF16), win_ref[...])

    def contract(c):
        hmid = (_silu(c["au"][:, :f]) * c["au"][:, f:]).astype(BF16)
        c["y"] = _dot(hmid, wout_ref[...])

    def finish(c):
        c["res"] = _post_norm(c["x"], c["y"], mod_ref, s, g_ref, b_ref, 0.5)

    stages = [expand, contract, finish]
    for step in range(len(stages) + FFN_SUBTILES - 1):
        for t in range(FFN_SUBTILES):
            if 0 <= step - t < len(stages):
                stages[step - t](st[t])
    res = jnp.concatenate([c["res"] for c in st], axis=0)
    if split_out:
        @pl.when(i < prompt_tiles)
        def _():
            o_refs[0][...] = res

        @pl.when(i >= prompt_tiles)
        def _():
            o_refs[1][...] = res
    else:
        o_refs[0][...] = res


def _ffn_call(xs, mod, w_in, w_out, g, b, *, s, prompt_rows, group_rows, split_out=False,
              next_w=None):
    tm = TM_FFN
    split_in = len(xs) == 2
    n = sum(x.shape[0] for x in xs)
    d = xs[0].shape[1]
    f = w_out.shape[0]
    n_tiles = n // tm
    prompt_tiles = prompt_rows // tm
    row = pl.BlockSpec((tm, d), lambda i: (i, 0))
    pair = _group_specs(tm, d, prompt_tiles)
    in_specs = (list(pair) if split_in else [row]) + [
        _mod_spec(d, group_rows // tm),
        _resident(w_in.shape),
        _resident(w_out.shape),
        _resident(g.shape),
        _resident(b.shape),
    ]
    args = list(xs) + [mod, w_in, w_out, g, b]
    if split_out:
        out_specs = list(pair)
        out_shape = [jax.ShapeDtypeStruct((prompt_rows, d), F32),
                     jax.ShapeDtypeStruct((n - prompt_rows, d), F32)]
    else:
        out_specs = [row]
        out_shape = [jax.ShapeDtypeStruct((n, d), F32)]
    cast_blocks = ()
    if next_w is not None:
        nw_in, nw_out, layer, posn = next_w
        c_in, c_out, c_shape, c_args, cast_blocks = _cast_jobs(
            [(nw_in, (layer, posn), 2 * BF16_SUBLANES),
             (nw_out, (layer, posn), 8 * BF16_SUBLANES)], n_tiles)
        in_specs += c_in
        out_specs += c_out
        out_shape += c_shape
        args += c_args
    kern = functools.partial(_ffn_kernel, s=s, f=f, split_in=split_in, split_out=split_out,
                             prompt_tiles=prompt_tiles, cast_blocks=cast_blocks)
    return pl.pallas_call(
        kern,
        grid=(n_tiles,),
        in_specs=in_specs,
        out_specs=out_specs,
        out_shape=out_shape,
        compiler_params=_params(("arbitrary",)),
        name=f"ffn_s{s}",
    )(*args)


def _mup_kernel(*refs, di, cast_blocks):
    x_ref, mod_ref, w_ref = refs[:3]
    n_cast = len(cast_blocks)
    xm_ref, z_ref = refs[3 + n_cast:5 + n_cast]
    _cast_step(pl.program_id(0), refs[3:3 + n_cast], refs[5 + n_cast:], cast_blocks)
    xmod = _modulate(x_ref[...], mod_ref, 1).astype(BF16)
    r = _dot(xmod, w_ref[...])
    xm_ref[...] = r[:, :di]
    z_ref[...] = r[:, di:].astype(z_ref.dtype)


def _mup_call(x, mod, w_up, later_weights, *, group_rows):
    tm = TM_UP
    n, d = x.shape
    di = w_up.shape[1] // 2
    out = pl.BlockSpec((tm, di), lambda i: (i, 0))
    c_in, c_out, c_shape, c_args, cast_blocks = _cast_jobs(later_weights, n // tm)
    return pl.pallas_call(
        functools.partial(_mup_kernel, di=di, cast_blocks=cast_blocks),
        grid=(n // tm,),
        in_specs=[
            pl.BlockSpec((tm, d), lambda i: (i, 0)),
            _mod_spec(d, group_rows // tm),
            _resident(w_up.shape),
        ] + c_in,
        out_specs=[out, out] + c_out,
        out_shape=[jax.ShapeDtypeStruct((n, di), F32), jax.ShapeDtypeStruct((n, di), BF16)]
        + c_shape,
        compiler_params=_params(("arbitrary",)),
        name="mlstm_up",
    )(x, mod, w_up, *c_args)


def _mqk_kernel(xm_ref, prev_ref, next_ref, wc_ref, bc_ref, wqk_ref, wg_ref,
                xc_ref, q_ref, k_ref, pre_ref, ext_ref, *, tm, di, conv_w, prompt_tiles,
                prompt_seq, sample_seq):
    i = pl.program_id(0)
    halo = SUBLANES
    seq = jnp.where(i < prompt_tiles, prompt_seq, sample_seq)
    starts_seq = ((i * tm) & (seq - 1)) == 0
    ends_seq = ((i * tm + tm) & (seq - 1)) == 0
    ext_ref[0:halo, :] = jnp.where(starts_seq, 0.0, prev_ref[...])
    ext_ref[halo:halo + tm, :] = xm_ref[...]
    ext_ref[halo + tm:2 * halo + tm, :] = jnp.where(ends_seq, 0.0, next_ref[...])
    groups = tm // SUBLANES
    acc = jnp.broadcast_to(bc_ref[...], (groups, SUBLANES, di))
    for j in range(conv_w):
        off = j - conv_w // 2
        tap = ext_ref[halo + off:halo + off + tm, :].reshape(groups, SUBLANES, di)
        acc = acc + tap * wc_ref[j]
    xc = _silu(acc.reshape(tm, di)).astype(BF16)
    xc_ref[...] = xc
    qk = _dot(xc, wqk_ref[...]).astype(BF16)
    q_ref[...] = qk[:, :di]
    k_ref[...] = qk[:, di:]
    pre_ref[...] = _dot(qk, wg_ref[...])


def _mqk_call(xm, w_conv, b_conv, w_qk, wg_qk, *, prompt_rows, prompt_seq, sample_seq):
    tm = TM_MIX
    n, di = xm.shape
    assert prompt_seq % tm == 0 and sample_seq % tm == 0 and w_conv.shape[0] // 2 <= SUBLANES
    halo_blocks = n // SUBLANES
    per_tile = tm // SUBLANES
    row = pl.BlockSpec((tm, di), lambda i: (i, 0))
    kern = functools.partial(
        _mqk_kernel, tm=tm, di=di, conv_w=w_conv.shape[0], prompt_tiles=prompt_rows // tm,
        prompt_seq=prompt_seq, sample_seq=sample_seq)
    return pl.pallas_call(
        kern,
        grid=(n // tm,),
        in_specs=[
            row,
            pl.BlockSpec((SUBLANES, di), lambda i: (jnp.maximum(i * per_tile - 1, 0), 0)),
            pl.BlockSpec((SUBLANES, di),
                         lambda i: (jnp.minimum((i + 1) * per_tile, halo_blocks - 1), 0)),
            _resident(w_conv.shape),
            _resident(b_conv.shape),
            _resident(w_qk.shape),
            _resident(wg_qk.shape),
        ],
        out_specs=[row, row, row, pl.BlockSpec((tm, LANES), lambda i: (i, 0))],
        out_shape=[
            jax.ShapeDtypeStruct((n, di), BF16),
            jax.ShapeDtypeStruct((n, di), BF16),
            jax.ShapeDtypeStruct((n, di), BF16),
            jax.ShapeDtypeStruct((n, LANES), F32),
        ],
        scratch_shapes=[pltpu.VMEM((tm + 2 * SUBLANES, di), F32)],
        compiler_params=_params(("arbitrary",)),
        name="mlstm_conv_qk",
    )(xm, xm, xm, w_conv, b_conv, w_qk, wg_qk)


def _mvo_kernel(xm_ref, pre_ref, wv_ref, wo_ref, bo_ref, wg_ref, bg_ref, v_ref, o_ref, preo_ref,
                *, sub):
    n_sub = xm_ref.shape[0] // sub
    st = [dict(rows=slice(t * sub, (t + 1) * sub)) for t in range(n_sub)]

    def project(s):
        xm = xm_ref[s["rows"], :].astype(BF16)
        s["v"] = _dot(xm, wv_ref[...])
        s["o"] = _dot(xm, wo_ref[...])

    def finish(s):
        rows = s["rows"]
        v = s["v"].astype(BF16)
        v_ref[rows, :] = v
        o_ref[rows, :] = _sigmoid(s["o"] + bo_ref[...]).astype(o_ref.dtype)
        preo_ref[rows, :] = pre_ref[rows, :] + _dot(v, wg_ref[...]) + bg_ref[...]

    stages = [project, finish]
    for step in range(len(stages) + n_sub - 1):
        for t in range(n_sub):
            if 0 <= step - t < len(stages):
                stages[step - t](st[t])


def _mvo_call(xm, pre_qk, w_v, w_o, b_o, wg_v, bg):
    tm = MVO_SUBTILES * TM_MIX
    n, di = xm.shape
    row = pl.BlockSpec((tm, di), lambda i: (i, 0))
    gate = pl.BlockSpec((tm, LANES), lambda i: (i, 0))
    return pl.pallas_call(
        functools.partial(_mvo_kernel, sub=TM_MIX),
        grid=(n // tm,),
        in_specs=[row, gate, _resident(w_v.shape), _resident(w_o.shape), _resident(b_o.shape),
                  _resident(wg_v.shape), _resident(bg.shape)],
        out_specs=[row, row, gate],
        out_shape=[
            jax.ShapeDtypeStruct((n, di), BF16),
            jax.ShapeDtypeStruct((n, di), BF16),
            jax.ShapeDtypeStruct((n, LANES), F32),
        ],
        compiler_params=_params(("arbitrary",)),
        name="mlstm_v_ogate",
    )(xm, pre_qk, w_v, w_o, b_o, wg_v, bg)


def _mlstm_core_kernel(*refs, nc, dh, chunk, has_state, emit_state):
    refs = list(refs)
    q_ref, k_ref, v_ref, g_ref = refs[:4]
    pos = 4
    if has_state:
        c0_ref, n0_ref, m0_ref = refs[pos:pos + 3]
        pos += 3
    h_ref = refs[pos]
    pos += 1
    if emit_state:
        cn_ref, nn_ref, mn_ref = refs[pos:pos + 3]
        pos += 3
    c_sc, cb_sc, n_sc, m_sc = refs[pos:pos + 4]

    d = pl.program_id(1)
    c = pl.program_id(2)
    L = chunk
    qscale = dh ** -0.5
    fresh = nc == 1 and not has_state

    if not fresh:
        @pl.when(c == 0)
        def _():
            if has_state:
                c_sc[...] = c0_ref[0, 0, 0]
                cb_sc[...] = c0_ref[0, 0, 0].astype(BF16)
                n_sc[...] = n0_ref[0, 0, 0]
                m_sc[...] = m0_ref[0]
            else:
                c_sc[...] = jnp.zeros_like(c_sc)
                cb_sc[...] = jnp.zeros_like(cb_sc)
                n_sc[...] = jnp.zeros_like(n_sc)
                m_sc[...] = jnp.zeros_like(m_sc)
    elif emit_state:
        mn_ref[0] = jnp.zeros(mn_ref.shape[1:], F32)

    gates = g_ref[...]
    gates_t = gates.T
    fwd = d == 0
    t_idx = lax.broadcasted_iota(jnp.int32, (L, L), 0)
    s_idx = lax.broadcasted_iota(jnp.int32, (L, L), 1)
    sgn = 1 - 2 * d
    causal = (t_idx - s_idx) * sgn >= 0
    causal_t = (s_idx - t_idx) * sgn >= 0

    heads = range(M_HEADS)
    lanes = [slice(h * dh, (h + 1) * dh) for h in heads]
    st = [dict() for _ in heads]
    for h, s in zip(heads, st):
        li, lf = h, M_HEADS + h
        bi, bf = 2 * M_HEADS + h, 3 * M_HEADS + h
        s["i_col"] = jnp.where(fwd, gates[:, li:li + 1], gates[:, bi:bi + 1])
        f_col = _log_sigmoid(jnp.where(fwd, gates[:, lf:lf + 1], gates[:, bf:bf + 1]))
        i_row = jnp.where(fwd, gates_t[li:li + 1, :], gates_t[bi:bi + 1, :])
        s["f_row"] = _log_sigmoid(jnp.where(fwd, gates_t[lf:lf + 1, :], gates_t[bf:bf + 1, :]))
        s["b_col"] = jnp.sum(jnp.where(causal, s["f_row"], 0.0), axis=1, keepdims=True)
        b_row = jnp.sum(jnp.where(causal_t, f_col, 0.0), axis=0, keepdims=True)
        s["m_prev"] = 0.0 if fresh else m_sc[:, h:h + 1]
        s["dmat"] = jnp.where(causal, s["b_col"] - b_row + i_row, -jnp.inf)
        s["inter"] = s["b_col"] + s["m_prev"]
        s["mr"] = jnp.maximum(jnp.max(s["dmat"], axis=1, keepdims=True), s["inter"])
    for h, s in zip(heads, st):
        s["qk"] = lax.dot_general(q_ref[:, lanes[h]], k_ref[:, lanes[h]],
                                  (((1,), (1,)), ((), ())), preferred_element_type=F32)
    for h, s in zip(heads, st):
        s["mr_q"] = s["mr"] - math.log(qscale)
        s["p"] = s["qk"] * jnp.exp(s["dmat"] - s["mr_q"])
        s["den"] = jnp.sum(s["p"], axis=1, keepdims=True)
    for h, s in zip(heads, st):
        s["num"] = _dot(s["p"].astype(BF16), v_ref[:, lanes[h]])
        if not fresh:
            s["qc"] = _dot(q_ref[:, lanes[h]], cb_sc[h])
            n_rows = jnp.broadcast_to(n_sc[h:h + 1, :], (SUBLANES, dh)).astype(BF16)
            s["qn"] = lax.dot_general(q_ref[:, lanes[h]], n_rows, (((1,), (1,)), ((), ())),
                                      preferred_element_type=F32)[:, 0:1]
    for h, s in zip(heads, st):
        num, den = s["num"], s["den"]
        if not fresh:
            w_in = jnp.exp(s["inter"] - s["mr_q"])
            num = num + w_in * s["qc"]
            den = den + w_in * s["qn"]
        hout = num * (1.0 / jnp.maximum(jnp.abs(den), jnp.exp(-s["mr"])))
        h_ref[0, :, lanes[h]] = hout.astype(h_ref.dtype)
    for h, s in zip(heads, st):
        g = jnp.sum(s["f_row"], axis=1, keepdims=True)
        lw = g - s["b_col"] + s["i_col"]
        s["m_new"] = jnp.maximum(g + s["m_prev"], jnp.max(lw, axis=0, keepdims=True))
        s["decay"] = jnp.exp(g + s["m_prev"] - s["m_new"])
        kw = k_ref[:, lanes[h]].astype(F32) * jnp.exp(lw - s["m_new"])
        s["n_new"] = jnp.sum(kw, axis=0, keepdims=True)
        s["kw"] = kw.astype(BF16)
    for h, s in zip(heads, st):
        s["c_new"] = lax.dot_general(s["kw"], v_ref[:, lanes[h]], (((0,), (0,)), ((), ())),
                                     preferred_element_type=F32)
    for h, s in zip(heads, st):
        if fresh:
            if emit_state:
                cn_ref[0, 0, 0, h] = s["c_new"]
                nn_ref[0, 0, 0, h:h + 1, :] = s["n_new"]
                mn_ref[0, :, h:h + 1] = s["m_new"]
        else:
            c_new = s["decay"] * c_sc[h] + s["c_new"]
            c_sc[h] = c_new
            cb_sc[h] = c_new.astype(BF16)
            n_sc[h:h + 1, :] = s["decay"] * n_sc[h:h + 1, :] + s["n_new"]
            m_sc[:, h:h + 1] = s["m_new"]

    if emit_state and not fresh:
        @pl.when(c == nc - 1)
        def _():
            cn_ref[0, 0, 0] = c_sc[...]
            nn_ref[0, 0, 0] = n_sc[...]
            mn_ref[0] = m_sc[...]


def _mlstm_core_call(q, k, v, pre, state, *, row0, batch, seq, emit_state):
    di = q.shape[1]
    dh = di // M_HEADS
    chunk = min(MLSTM_CHUNK, seq)
    nc = seq // chunk
    chunk0 = row0 // chunk
    has_state = state is not None

    def chunk_of(b, d, c):
        return b * nc + c + d * (nc - 1 - 2 * c)

    tok = pl.BlockSpec((chunk, di), lambda b, d, c: (chunk0 + chunk_of(b, d, c), 0))
    in_specs = [tok, tok, tok,
                pl.BlockSpec((chunk, LANES), lambda b, d, c: (chunk0 + chunk_of(b, d, c), 0))]
    args = [q, k, v, pre]
    state_c = pl.BlockSpec((1, 1, 1, M_HEADS, dh, dh), lambda b, d, c: (b, 0, d, 0, 0, 0))
    state_n = pl.BlockSpec((1, 1, 1, M_HEADS, dh), lambda b, d, c: (b, 0, d, 0, 0))
    state_m = pl.BlockSpec((1, 1, LANES), lambda b, d, c: (b * 2 + d, 0, 0))
    if has_state:
        in_specs += [state_c, state_n, state_m]
        args += list(state)
    out_specs = [pl.BlockSpec((1, chunk, di), lambda b, d, c: (d, chunk_of(b, d, c), 0))]
    out_shape = [jax.ShapeDtypeStruct((2, batch * seq, di), BF16)]
    if emit_state:
        out_specs += [state_c, state_n, state_m]
        out_shape += [
            jax.ShapeDtypeStruct((batch, 1, 2, M_HEADS, dh, dh), F32),
            jax.ShapeDtypeStruct((batch, 1, 2, M_HEADS, dh), F32),
            jax.ShapeDtypeStruct((batch * 2, 1, LANES), F32),
        ]
    kern = functools.partial(_mlstm_core_kernel, nc=nc, dh=dh, chunk=chunk,
                             has_state=has_state, emit_state=emit_state)
    return pl.pallas_call(
        kern,
        grid=(batch, 2, nc),
        in_specs=in_specs,
        out_specs=out_specs,
        out_shape=out_shape,
        scratch_shapes=[pltpu.VMEM((M_HEADS, dh, dh), F32), pltpu.VMEM((M_HEADS, dh, dh), BF16),
                        pltpu.VMEM((M_HEADS, dh), F32), pltpu.VMEM((1, LANES), F32)],
        compiler_params=_params(("arbitrary",) * 3),
        name="mlstm_core_state" if emit_state else "mlstm_core",
    )(*args)


def _mdown_kernel(x_ref, mod_ref, o_ref, hpf_ref, hpb_ref, hsf_ref, hsb_ref, xc_ref, z_ref,
                  ng_ref, skip_ref, wd_ref, g_ref, b_ref, out_ref, *, prompt_tiles, dh, sub):
    i = pl.program_id(0)
    is_prompt = i < prompt_tiles
    n_sub = x_ref.shape[0] // sub
    st = [dict(rows=slice(t * sub, (t + 1) * sub)) for t in range(n_sub)]

    def gate(s):
        rows = s["rows"]
        hsum = (jnp.where(is_prompt, hpf_ref[0, rows, :], hsf_ref[0, rows, :]).astype(F32)
                + jnp.where(is_prompt, hpb_ref[0, rows, :], hsb_ref[0, rows, :]).astype(F32))
        parts = []
        for h in range(M_HEADS):
            hh = hsum[:, h * dh:(h + 1) * dh]
            mu = jnp.mean(hh, axis=-1, keepdims=True)
            hc = hh - mu
            var = jnp.mean(hc * hc, axis=-1, keepdims=True)
            parts.append(hc * lax.rsqrt(var + LN_EPS))
        hn = jnp.concatenate(parts, axis=-1) * ng_ref[...]
        mixed = ((o_ref[rows, :].astype(F32) * hn
                  + skip_ref[...] * xc_ref[rows, :].astype(F32))
                 * _silu(z_ref[rows, :].astype(F32)))
        s["mixed"] = mixed.astype(BF16)

    def project(s):
        s["y"] = _dot(s["mixed"], wd_ref[...])

    def finish(s):
        rows = s["rows"]
        out_ref[rows, :] = _post_norm(x_ref[rows, :], s["y"], mod_ref, 1, g_ref, b_ref, 1.0)

    stages = [gate, project, finish]
    for step in range(len(stages) + n_sub - 1):
        for t in range(n_sub):
            if 0 <= step - t < len(stages):
                stages[step - t](st[t])


def _mdown_call(x, mod, o, h_p, h_s, xc, z, norm_g, skip, w_down, g, b, *, group_rows):
    tm = MDOWN_SUBTILES * TM_MIX
    n, d = x.shape
    di = o.shape[1]
    prompt_tiles = h_p.shape[1] // tm
    row = pl.BlockSpec((tm, di), lambda i: (i, 0))
    xrow = pl.BlockSpec((tm, d), lambda i: (i, 0))

    def hspec(direction, first_group):
        if first_group:
            return pl.BlockSpec((1, tm, di),
                                lambda i: (direction, jnp.minimum(i, prompt_tiles - 1), 0))
        return pl.BlockSpec((1, tm, di),
                            lambda i: (direction, jnp.maximum(i - prompt_tiles, 0), 0))

    consts = [norm_g, skip, w_down, g, b]
    return pl.pallas_call(
        functools.partial(_mdown_kernel, prompt_tiles=prompt_tiles, dh=di // M_HEADS,
                          sub=TM_MIX),
        grid=(n // tm,),
        in_specs=[xrow, _mod_spec(d, group_rows // tm), row,
                  hspec(0, True), hspec(1, True), hspec(0, False), hspec(1, False), row, row]
        + [_resident(a.shape) for a in consts],
        out_specs=xrow,
        out_shape=jax.ShapeDtypeStruct((n, d), F32),
        compiler_params=_params(("arbitrary",)),
        name="mlstm_down",
    )(x, mod, o, h_p, h_p, h_s, h_s, xc, z, *consts)


def _conf_kernel(x_ref, mod_ref, w1_ref, b1_ref, wdw_ref, bdw_ref, lg_ref, lb_ref, w2_ref,
                 b2_ref, g_ref, b_ref, out_ref, sh_ref, conv_ref, *, tm, d, conv_w, seg_len):
    half = conv_w // 2
    gap = 2 * SUBLANES
    assert half < gap
    n_seg = tm // seg_len
    stride = seg_len + gap
    n_rows = gap + n_seg * stride
    conv_rows = 4 * SUBLANES

    groups = conv_rows // SUBLANES
    n_sub = x_ref.shape[0] // tm
    st = [dict(rows=slice(t * tm, (t + 1) * tm)) for t in range(n_sub)]

    def pointwise_in(t, s):
        s["x"] = x_ref[s["rows"], :]
        xm = _modulate(s["x"], mod_ref, 1).astype(BF16)
        s["ag"] = _dot(xm, w1_ref[...]) + b1_ref[...]

    def lay_out(t, s):
        glu = s["ag"][:, :d] * _sigmoid(s["ag"][:, d:])
        for g in range(n_seg + 1):
            sh_ref[t, 0, g * stride:g * stride + gap, :] = jnp.zeros((gap, d), F32)
        for g in range(n_seg):
            sh_ref[t, 0, gap + g * stride:gap + g * stride + seg_len, :] = (
                glu[g * seg_len:(g + 1) * seg_len, :])
        for r in range(1, SUBLANES):
            sh_ref[t, r, 0:n_rows - SUBLANES, :] = sh_ref[t, 0, r:r + n_rows - SUBLANES, :]

    def taps(t, s):
        for g in range(n_seg):
            for r0 in range(0, seg_len, conv_rows):
                acc = jnp.broadcast_to(bdw_ref[...], (groups, SUBLANES, d))
                for j in range(conv_w):
                    r = (j - half) % SUBLANES
                    y = gap + g * stride + r0 + (j - half) - r
                    tap = sh_ref[t, r, y:y + conv_rows, :].reshape(groups, SUBLANES, d)
                    acc = acc + tap * wdw_ref[j]
                c0 = g * seg_len + r0
                conv_ref[t, c0:c0 + conv_rows, :] = acc.reshape(conv_rows, d)

    def pointwise_out(t, s):
        hact = _silu(_layer_norm(conv_ref[t], lg_ref[...], lb_ref[...]))
        s["y"] = _dot(hact.astype(BF16), w2_ref[...]) + b2_ref[...]

    def finish(t, s):
        out_ref[s["rows"], :] = _post_norm(s["x"], s["y"], mod_ref, 1, g_ref, b_ref, 1.0)

    stages = [pointwise_in, lay_out, taps, pointwise_out, finish]
    for step in range(len(stages) + n_sub - 1):
        for t in range(n_sub):
            k = step - t
            if 0 <= k < len(stages):
                stages[k](t, st[t])


def _conf_call(x, mod, w1, b1, wdw, bdw, lg, lb, w2, b2, g, b, *, group_rows, row0, rows,
               seg_len):
    tm = TM_MIX
    blk = CONF_SUBTILES * tm
    _, d = x.shape
    assert tm % seg_len == 0 and seg_len % (4 * SUBLANES) == 0
    assert row0 % blk == 0 and rows % blk == 0 and group_rows % blk == 0
    tile0 = row0 // blk
    gap = 2 * SUBLANES
    n_rows = gap + (tm // seg_len) * (seg_len + gap)
    kern = functools.partial(_conf_kernel, tm=tm, d=d, conv_w=wdw.shape[0], seg_len=seg_len)
    consts = [w1, b1, wdw, bdw, lg, lb, w2, b2, g, b]
    return pl.pallas_call(
        kern,
        grid=(rows // blk,),
        in_specs=[pl.BlockSpec((blk, d), lambda i: (tile0 + i, 0)),
                  pl.BlockSpec((1, N_MOD, d),
                               lambda i: ((tile0 + i) // (group_rows // blk), 0, 0))]
        + [_resident(a.shape) for a in consts],
        out_specs=pl.BlockSpec((blk, d), lambda i: (i, 0)),
        out_shape=jax.ShapeDtypeStruct((rows, d), F32),
        scratch_shapes=[pltpu.VMEM((CONF_SUBTILES, SUBLANES, n_rows, d), F32),
                        pltpu.VMEM((CONF_SUBTILES, tm, d), F32)],
        compiler_params=_params(("arbitrary",)),
        name=f"conformer_conv_seg{seg_len}",
    )(x, mod, *consts)


def _row(a):
    return a.reshape(1, -1)


def kernel(x_prompt, x_sample, state_C, state_n, state_m, c, c_ctx, w_mod, b_mod, ln_g, ln_b, ffn_w_in, ffn_w_out, m_w_up, m_w_conv, m_b_conv, m_w_qk, m_w_v, m_w_gate, m_b_gate, m_w_o, m_b_o, m_norm_g, m_skip, m_w_down, cv_w_pw1, cv_b_pw1, cv_w_dw, cv_b_dw, cv_ln_g, cv_ln_b, cv_w_pw2, cv_b_pw2):
    bp, sp, d = x_prompt.shape
    bs, ss, _ = x_sample.shape
    n_p, n_s = bp * sp, bs * ss
    heads = M_HEADS
    group_rows = ss
    assert n_p == group_rows and sp & (sp - 1) == 0 and ss & (ss - 1) == 0
    assert 1 + bs <= COND_ROWS and GRID_W & (GRID_W - 1) == 0
    assert m_w_up.shape[0] == 1 and 4 * heads <= LANES

    cond = jnp.concatenate([c_ctx[None], c, jnp.zeros((COND_ROWS - 1 - bs, d), F32)], axis=0)
    mod_all = _mod_call(cond, w_mod, b_mod).reshape(DEPTH, COND_ROWS, N_MOD, d)

    xs = (x_prompt.reshape(n_p, d), x_sample.reshape(n_s, d))
    new_c = new_n = new_m = None
    ffn_w = (ffn_w_in[0, 0].astype(BF16), ffn_w_out[0, 0].astype(BF16))
    for i in range(DEPTH):
        mod = mod_all[i]
        ffn = functools.partial(_ffn_call, prompt_rows=n_p, group_rows=group_rows)
        x, *ffn_w = ffn(xs, mod, *ffn_w, _row(ln_g[i, 0]), _row(ln_b[i, 0]), s=0,
                        next_w=(ffn_w_in, ffn_w_out, i, 1))
        j = i // 2
        if i % 2 == 0:
            di = m_w_v.shape[1]
            cast_rows = 4 * BF16_SUBLANES
            xm, z, w_qk, w_v, w_o, w_down = _mup_call(
                x, mod, m_w_up[j].astype(BF16),
                [(w, (j,), cast_rows) for w in (m_w_qk, m_w_v, m_w_o, m_w_down)],
                group_rows=group_rows)
            n_gate = m_w_gate.shape[2]
            wg = jnp.pad(m_w_gate[j], ((0, 0), (0, LANES - n_gate))).astype(BF16)
            bg = jnp.pad(m_b_gate[j], (0, LANES - n_gate)).reshape(1, LANES)
            w_conv = jnp.broadcast_to(m_w_conv[j][:, None, :],
                                      (m_w_conv.shape[1], SUBLANES, di))
            xc, q, k, pre_qk = _mqk_call(
                xm, w_conv, _row(m_b_conv[j]), w_qk, wg[:2 * di],
                prompt_rows=n_p, prompt_seq=sp, sample_seq=ss)
            v, o, pre = _mvo_call(xm, pre_qk, w_v, w_o, _row(m_b_o[j]), wg[2 * di:], bg)
            h_p, new_c, new_n, m_p = _mlstm_core_call(
                q, k, v, pre, None, row0=0, batch=bp, seq=sp, emit_state=True)
            m0 = jnp.pad(state_m[:, j].reshape(bs * 2, 1, heads),
                         ((0, 0), (0, 0), (0, LANES - heads)))
            (h_s,) = _mlstm_core_call(
                q, k, v, pre, (state_C[:, j:j + 1], state_n[:, j:j + 1], m0),
                row0=n_p, batch=bs, seq=ss, emit_state=False)
            x = (_mdown_call(x, mod, o, h_p, h_s, xc, z, _row(m_norm_g[j]), _row(m_skip[j]),
                             w_down, _row(ln_g[i, 1]), _row(ln_b[i, 1]),
                             group_rows=group_rows),)
            new_m = m_p[:, 0, :heads].reshape(bp, 1, 2, heads)
        else:
            conf = functools.partial(
                _conf_call, x, mod, cv_w_pw1[j].astype(BF16), _row(cv_b_pw1[j]),
                jnp.broadcast_to(cv_w_dw[j][:, None, :], (cv_w_dw.shape[1], SUBLANES, d)),
                _row(cv_b_dw[j]), _row(cv_ln_g[j]), _row(cv_ln_b[j]), cv_w_pw2[j].astype(BF16),
                _row(cv_b_pw2[j]), _row(ln_g[i, 1]), _row(ln_b[i, 1]), group_rows=group_rows)
            x = (conf(row0=0, rows=n_p, seg_len=sp), conf(row0=n_p, rows=n_s, seg_len=GRID_W))
        last = i == DEPTH - 1
        outs = ffn(x, mod, *ffn_w, _row(ln_g[i, 2]), _row(ln_b[i, 2]), s=2, split_out=last,
                   next_w=None if last else (ffn_w_in, ffn_w_out, i + 1, 0))
        xs, ffn_w = (outs, None) if last else (outs[:1], outs[1:])
    y_p, y_s = xs
    return (y_p.reshape(bp, sp, d), y_s.reshape(bs, ss, d), new_c, new_n, new_m)
```

```python
import functools
import math

import jax
import jax.numpy as jnp
from jax import lax
from jax.experimental import pallas as pl
from jax.experimental.pallas import tpu as pltpu

F32 = jnp.float32
BF16 = jnp.bfloat16

DEPTH = 2
GRID_W = 64
M_HEADS = 4
N_SUB = 3
N_MOD = 3 * N_SUB
ALPHA = (2 * DEPTH) ** 0.25
LN_EPS = 1e-5

SUBLANES = 8
BF16_SUBLANES = 16
LANES = 128
VMEM_LIMIT = 56 * 1024 * 1024

COND_ROWS = 8

TM_FFN = 512
TM_UP = 512
TM_MIX = 256
CONF_SUBTILES = 2
FFN_SUBTILES = 2
MDOWN_SUBTILES = 2
MVO_SUBTILES = 2
MLSTM_CHUNK = 256
MLSTM_CHUNKS_PER_STEP = 2


def _resident(shape):
    nd = len(shape)
    return pl.BlockSpec(shape, lambda *_: (0,) * nd, pipeline_mode=pl.Buffered(1))


def _params(sem):
    return pltpu.CompilerParams(dimension_semantics=sem, vmem_limit_bytes=VMEM_LIMIT)


def _sigmoid(x):
    return 1.0 / (1.0 + jnp.exp(-x))


def _silu(x):
    return x * _sigmoid(x)


def _log_sigmoid(x):
    return jnp.minimum(x, 0.0) - jnp.log1p(jnp.exp(-jnp.abs(x)))


def _layer_norm(x, g, b):
    mu = jnp.mean(x, axis=-1, keepdims=True)
    xc = x - mu
    var = jnp.mean(xc * xc, axis=-1, keepdims=True)
    return xc * lax.rsqrt(var + LN_EPS) * g + b


def _modulate(x, mod_ref, s):
    shift = mod_ref[0, 3 * s:3 * s + 1, :]
    scale = mod_ref[0, 3 * s + 1:3 * s + 2, :]
    return x * (1.0 + scale) + shift


def _post_norm(x, y, mod_ref, s, g_ref, b_ref, res_w):
    gate = mod_ref[0, 3 * s + 2:3 * s + 3, :]
    return _layer_norm(ALPHA * x + (res_w * gate) * y, g_ref[...], b_ref[...])


def _dot(a, b):
    return jnp.dot(a, b, preferred_element_type=F32)


def _mod_spec(d, tiles_per_group):
    return pl.BlockSpec((1, N_MOD, d), lambda i: (i // tiles_per_group, 0, 0))


def _group_specs(tm, width, prompt_tiles):
    return (pl.BlockSpec((tm, width), lambda i: (jnp.minimum(i, prompt_tiles - 1), 0)),
            pl.BlockSpec((tm, width), lambda i: (jnp.maximum(i - prompt_tiles, 0), 0)))


def _cast_jobs(weights, n_steps):
    in_specs, out_specs, out_shape, args, blocks = [], [], [], [], []
    for w, lead, rows in weights:
        r, c = w.shape[-2:]
        assert r % rows == 0 and rows % BF16_SUBLANES == 0 and r // rows <= n_steps
        nb = r // rows
        in_specs.append(pl.BlockSpec(
            (1,) * len(lead) + (rows, c),
            lambda i, lead=lead, nb=nb: (*lead, jnp.minimum(i, nb - 1), 0)))
        out_specs.append(pl.BlockSpec((rows, c), lambda i, nb=nb: (jnp.minimum(i, nb - 1), 0)))
        out_shape.append(jax.ShapeDtypeStruct((r, c), BF16))
        args.append(w)
        blocks.append(nb)
    return in_specs, out_specs, out_shape, args, tuple(blocks)


def _cast_step(i, in_refs, out_refs, blocks):
    for src, dst, nb in zip(in_refs, out_refs, blocks):
        lead = (0,) * (len(src.shape) - 2)

        @pl.when(i < nb)
        def _(src=src, dst=dst, lead=lead):
            dst[...] = src[lead].astype(BF16)


def _mod_kernel(c_ref, w_ref, b_ref, o_ref):
    cond = _silu(c_ref[...])
    o_ref[0] = _dot(cond.astype(BF16), w_ref[0].astype(BF16)) + b_ref[0]


def _mod_call(cond, w_mod, b_mod):
    depth, d, n = w_mod.shape
    tn = n // 8
    return pl.pallas_call(
        _mod_kernel,
        grid=(depth, n // tn),
        in_specs=[
            pl.BlockSpec((COND_ROWS, d), lambda l, j: (0, 0)),
            pl.BlockSpec((1, d, tn), lambda l, j: (l, 0, j)),
            pl.BlockSpec((1, 1, tn), lambda l, j: (l, 0, j)),
        ],
        out_specs=pl.BlockSpec((1, COND_ROWS, tn), lambda l, j: (l, 0, j)),
        out_shape=jax.ShapeDtypeStruct((depth, COND_ROWS, n), F32),
        compiler_params=_params(("arbitrary", "arbitrary")),
        name="adaln_mod",
    )(cond, w_mod, b_mod.reshape(depth, 1, n))


def _ffn_kernel(*refs, s, f, split_in, split_out, prompt_tiles, cast_blocks):
    refs = list(refs)
    n_x = 2 if split_in else 1
    x_refs = refs[:n_x]
    mod_ref, win_ref, wout_ref, g_ref, b_ref = refs[n_x:n_x + 5]
    pos = n_x + 5
    n_cast = len(cast_blocks)
    cast_in = refs[pos:pos + n_cast]
    pos += n_cast
    o_refs = refs[pos:pos + (2 if split_out else 1)]
    pos += len(o_refs)
    i = pl.program_id(0)
    _cast_step(i, cast_in, refs[pos:pos + n_cast], cast_blocks)

    sub = x_refs[0].shape[0] // FFN_SUBTILES
    st = [dict(rows=slice(t * sub, (t + 1) * sub)) for t in range(FFN_SUBTILES)]

    def expand(c):
        rows = c["rows"]
        if split_in:
            c["x"] = jnp.where(i < prompt_tiles, x_refs[0][rows, :], x_refs[1][rows, :])
        else:
            c["x"] = x_refs[0][rows, :]
        c["au"] = _dot(_modulate(c["x"], mod_ref, s).astype(BF16), win_ref[...])

    def contract(c):
        hmid = (_silu(c["au"][:, :f]) * c["au"][:, f:]).astype(BF16)
        c["y"] = _dot(hmid, wout_ref[...])

    def finish(c):
        c["res"] = _post_norm(c["x"], c["y"], mod_ref, s, g_ref, b_ref, 0.5)

    stages = [expand, contract, finish]
    for step in range(len(stages) + FFN_SUBTILES - 1):
        for t in range(FFN_SUBTILES):
            if 0 <= step - t < len(stages):
                stages[step - t](st[t])
    res = jnp.concatenate([c["res"] for c in st], axis=0)
    if split_out:
        @pl.when(i < prompt_tiles)
        def _():
            o_refs[0][...] = res

        @pl.when(i >= prompt_tiles)
        def _():
            o_refs[1][...] = res
    else:
        o_refs[0][...] = res


def _ffn_call(xs, mod, w_in, w_out, g, b, *, s, prompt_rows, group_rows, split_out=False,
              next_w=None):
    tm = TM_FFN
    split_in = len(xs) == 2
    n = sum(x.shape[0] for x in xs)
    d = xs[0].shape[1]
    f = w_out.shape[0]
    n_tiles = n // tm
    prompt_tiles = prompt_rows // tm
    row = pl.BlockSpec((tm, d), lambda i: (i, 0))
    pair = _group_specs(tm, d, prompt_tiles)
    in_specs = (list(pair) if split_in else [row]) + [
        _mod_spec(d, group_rows // tm),
        _resident(w_in.shape),
        _resident(w_out.shape),
        _resident(g.shape),
        _resident(b.shape),
    ]
    args = list(xs) + [mod, w_in, w_out, g, b]
    if split_out:
        out_specs = list(pair)
        out_shape = [jax.ShapeDtypeStruct((prompt_rows, d), F32),
                     jax.ShapeDtypeStruct((n - prompt_rows, d), F32)]
    else:
        out_specs = [row]
        out_shape = [jax.ShapeDtypeStruct((n, d), F32)]
    cast_blocks = ()
    if next_w is not None:
        nw_in, nw_out, layer, posn = next_w
        c_in, c_out, c_shape, c_args, cast_blocks = _cast_jobs(
            [(nw_in, (layer, posn), 2 * BF16_SUBLANES),
             (nw_out, (layer, posn), 8 * BF16_SUBLANES)], n_tiles)
        in_specs += c_in
        out_specs += c_out
        out_shape += c_shape
        args += c_args
    kern = functools.partial(_ffn_kernel, s=s, f=f, split_in=split_in, split_out=split_out,
                             prompt_tiles=prompt_tiles, cast_blocks=cast_blocks)
    return pl.pallas_call(
        kern,
        grid=(n_tiles,),
        in_specs=in_specs,
        out_specs=out_specs,
        out_shape=out_shape,
        compiler_params=_params(("arbitrary",)),
        name=f"ffn_s{s}",
    )(*args)


def _mup_kernel(*refs, di, cast_blocks):
    x_ref, mod_ref, w_ref = refs[:3]
    n_cast = len(cast_blocks)
    xm_ref, z_ref = refs[3 + n_cast:5 + n_cast]
    _cast_step(pl.program_id(0), refs[3:3 + n_cast], refs[5 + n_cast:], cast_blocks)
    xmod = _modulate(x_ref[...], mod_ref, 1).astype(BF16)
    r = _dot(xmod, w_ref[...])
    xm_ref[...] = r[:, :di]
    z_ref[...] = r[:, di:].astype(z_ref.dtype)


def _mup_call(x, mod, w_up, later_weights, *, group_rows):
    tm = TM_UP
    n, d = x.shape
    di = w_up.shape[1] // 2
    out = pl.BlockSpec((tm, di), lambda i: (i, 0))
    c_in, c_out, c_shape, c_args, cast_blocks = _cast_jobs(later_weights, n // tm)
    return pl.pallas_call(
        functools.partial(_mup_kernel, di=di, cast_blocks=cast_blocks),
        grid=(n // tm,),
        in_specs=[
            pl.BlockSpec((tm, d), lambda i: (i, 0)),
            _mod_spec(d, group_rows // tm),
            _resident(w_up.shape),
        ] + c_in,
        out_specs=[out, out] + c_out,
        out_shape=[jax.ShapeDtypeStruct((n, di), F32), jax.ShapeDtypeStruct((n, di), BF16)]
        + c_shape,
        compiler_params=_params(("arbitrary",)),
        name="mlstm_up",
    )(x, mod, w_up, *c_args)


def _mqk_kernel(xm_ref, prev_ref, next_ref, wc_ref, bc_ref, wqk_ref, wg_ref,
                xc_ref, q_ref, k_ref, pre_ref, ext_ref, *, tm, di, conv_w, prompt_tiles,
                prompt_seq, sample_seq):
    i = pl.program_id(0)
    halo = SUBLANES
    seq = jnp.where(i < prompt_tiles, prompt_seq, sample_seq)
    starts_seq = ((i * tm) & (seq - 1)) == 0
    ends_seq = ((i * tm + tm) & (seq - 1)) == 0
    ext_ref[0:halo, :] = jnp.where(starts_seq, 0.0, prev_ref[...])
    ext_ref[halo:halo + tm, :] = xm_ref[...]
    ext_ref[halo + tm:2 * halo + tm, :] = jnp.where(ends_seq, 0.0, next_ref[...])
    groups = tm // SUBLANES
    acc = jnp.broadcast_to(bc_ref[...], (groups, SUBLANES, di))
    for j in range(conv_w):
        off = j - conv_w // 2
        tap = ext_ref[halo + off:halo + off + tm, :].reshape(groups, SUBLANES, di)
        acc = acc + tap * wc_ref[j]
    xc = _silu(acc.reshape(tm, di)).astype(BF16)
    xc_ref[...] = xc
    qk = _dot(xc, wqk_ref[...]).astype(BF16)
    q_ref[...] = qk[:, :di]
    k_ref[...] = qk[:, di:]
    pre_ref[...] = _dot(qk, wg_ref[...])


def _mqk_call(xm, w_conv, b_conv, w_qk, wg_qk, *, prompt_rows, prompt_seq, sample_seq):
    tm = TM_MIX
    n, di = xm.shape
    assert prompt_seq % tm == 0 and sample_seq % tm == 0 and w_conv.shape[0] // 2 <= SUBLANES
    halo_blocks = n // SUBLANES
    per_tile = tm // SUBLANES
    row = pl.BlockSpec((tm, di), lambda i: (i, 0))
    kern = functools.partial(
        _mqk_kernel, tm=tm, di=di, conv_w=w_conv.shape[0], prompt_tiles=prompt_rows // tm,
        prompt_seq=prompt_seq, sample_seq=sample_seq)
    return pl.pallas_call(
        kern,
        grid=(n // tm,),
        in_specs=[
            row,
            pl.BlockSpec((SUBLANES, di), lambda i: (jnp.maximum(i * per_tile - 1, 0), 0)),
            pl.BlockSpec((SUBLANES, di),
                         lambda i: (jnp.minimum((i + 1) * per_tile, halo_blocks - 1), 0)),
            _resident(w_conv.shape),
            _resident(b_conv.shape),
            _resident(w_qk.shape),
            _resident(wg_qk.shape),
        ],
        out_specs=[row, row, row, pl.BlockSpec((tm, LANES), lambda i: (i, 0))],
        out_shape=[
            jax.ShapeDtypeStruct((n, di), BF16),
            jax.ShapeDtypeStruct((n, di), BF16),
            jax.ShapeDtypeStruct((n, di), BF16),
            jax.ShapeDtypeStruct((n, LANES), F32),
        ],
        scratch_shapes=[pltpu.VMEM((tm + 2 * SUBLANES, di), F32)],
        compiler_params=_params(("arbitrary",)),
        name="mlstm_conv_qk",
    )(xm, xm, xm, w_conv, b_conv, w_qk, wg_qk)


def _mvo_kernel(xm_ref, pre_ref, wv_ref, wo_ref, bo_ref, wg_ref, bg_ref, v_ref, o_ref, preo_ref,
                *, sub):
    n_sub = xm_ref.shape[0] // sub
    st = [dict(rows=slice(t * sub, (t + 1) * sub)) for t in range(n_sub)]

    def project(s):
        xm = xm_ref[s["rows"], :].astype(BF16)
        s["v"] = _dot(xm, wv_ref[...])
        s["o"] = _dot(xm, wo_ref[...])

    def finish(s):
        rows = s["rows"]
        v = s["v"].astype(BF16)
        v_ref[rows, :] = v
        o_ref[rows, :] = _sigmoid(s["o"] + bo_ref[...]).astype(o_ref.dtype)
        preo_ref[rows, :] = pre_ref[rows, :] + _dot(v, wg_ref[...]) + bg_ref[...]

    stages = [project, finish]
    for step in range(len(stages) + n_sub - 1):
        for t in range(n_sub):
            if 0 <= step - t < len(stages):
                stages[step - t](st[t])


def _mvo_call(xm, pre_qk, w_v, w_o, b_o, wg_v, bg):
    tm = MVO_SUBTILES * TM_MIX
    n, di = xm.shape
    row = pl.BlockSpec((tm, di), lambda i: (i, 0))
    gate = pl.BlockSpec((tm, LANES), lambda i: (i, 0))
    return pl.pallas_call(
        functools.partial(_mvo_kernel, sub=TM_MIX),
        grid=(n // tm,),
        in_specs=[row, gate, _resident(w_v.shape), _resident(w_o.shape), _resident(b_o.shape),
                  _resident(wg_v.shape), _resident(bg.shape)],
        out_specs=[row, row, gate],
        out_shape=[
            jax.ShapeDtypeStruct((n, di), BF16),
            jax.ShapeDtypeStruct((n, di), BF16),
            jax.ShapeDtypeStruct((n, LANES), F32),
        ],
        compiler_params=_params(("arbitrary",)),
        name="mlstm_v_ogate",
    )(xm, pre_qk, w_v, w_o, b_o, wg_v, bg)


def _mlstm_core_kernel(*refs, nc, dh, chunk, cps, has_state, emit_state):
    refs = list(refs)
    q_ref, k_ref, v_ref, g_ref = refs[:4]
    pos = 4
    if has_state:
        c0_ref, n0_ref, m0_ref = refs[pos:pos + 3]
        pos += 3
    h_ref = refs[pos]
    pos += 1
    if emit_state:
        cn_ref, nn_ref, mn_ref = refs[pos:pos + 3]
        pos += 3
    c_sc, cb_sc, n_sc, m_sc = refs[pos:pos + 4]

    d = pl.program_id(1)
    c = pl.program_id(2)
    L = chunk
    qscale = dh ** -0.5
    fresh = nc == 1 and not has_state

    if not fresh:
        @pl.when(c == 0)
        def _():
            if has_state:
                c_sc[...] = c0_ref[0, 0, 0]
                cb_sc[...] = c0_ref[0, 0, 0].astype(BF16)
                n_sc[...] = n0_ref[0, 0, 0]
                m_sc[...] = m0_ref[0]
            else:
                c_sc[...] = jnp.zeros_like(c_sc)
                cb_sc[...] = jnp.zeros_like(cb_sc)
                n_sc[...] = jnp.zeros_like(n_sc)
                m_sc[...] = jnp.zeros_like(m_sc)
    elif emit_state:
        mn_ref[0] = jnp.zeros(mn_ref.shape[1:], F32)

    fwd = d == 0
    for it in range(cps):
        sub = jnp.where(fwd, it, cps - 1 - it)
        rws = pl.ds(pl.multiple_of(sub * L, L), L)
        gates = g_ref[rws, :]
        gates_t = gates.T
        t_idx = lax.broadcasted_iota(jnp.int32, (L, L), 0)
        s_idx = lax.broadcasted_iota(jnp.int32, (L, L), 1)
        sgn = 1 - 2 * d
        causal = (t_idx - s_idx) * sgn >= 0
        causal_t = (s_idx - t_idx) * sgn >= 0

        heads = range(M_HEADS)
        lanes = [slice(h * dh, (h + 1) * dh) for h in heads]
        st = [dict() for _ in heads]
        for h, s in zip(heads, st):
            li, lf = h, M_HEADS + h
            bi, bf = 2 * M_HEADS + h, 3 * M_HEADS + h
            s["i_col"] = jnp.where(fwd, gates[:, li:li + 1], gates[:, bi:bi + 1])
            f_col = _log_sigmoid(jnp.where(fwd, gates[:, lf:lf + 1], gates[:, bf:bf + 1]))
            i_row = jnp.where(fwd, gates_t[li:li + 1, :], gates_t[bi:bi + 1, :])
            s["f_row"] = _log_sigmoid(jnp.where(fwd, gates_t[lf:lf + 1, :], gates_t[bf:bf + 1, :]))
            s["b_col"] = jnp.sum(jnp.where(causal, s["f_row"], 0.0), axis=1, keepdims=True)
            b_row = jnp.sum(jnp.where(causal_t, f_col, 0.0), axis=0, keepdims=True)
            s["m_prev"] = 0.0 if fresh else m_sc[:, h:h + 1]
            s["dmat"] = jnp.where(causal, s["b_col"] - b_row + i_row, -jnp.inf)
            s["inter"] = s["b_col"] + s["m_prev"]
            s["mr"] = jnp.maximum(jnp.max(s["dmat"], axis=1, keepdims=True), s["inter"])
        for h, s in zip(heads, st):
            s["qk"] = lax.dot_general(q_ref[rws, lanes[h]], k_ref[rws, lanes[h]],
                                      (((1,), (1,)), ((), ())), preferred_element_type=F32)
        for h, s in zip(heads, st):
            s["mr_q"] = s["mr"] - math.log(qscale)
            s["p"] = s["qk"] * jnp.exp(s["dmat"] - s["mr_q"])
            s["den"] = jnp.sum(s["p"], axis=1, keepdims=True)
        for h, s in zip(heads, st):
            s["num"] = _dot(s["p"].astype(BF16), v_ref[rws, lanes[h]])
            if not fresh:
                s["qc"] = _dot(q_ref[rws, lanes[h]], cb_sc[h])
                n_rows = jnp.broadcast_to(n_sc[h:h + 1, :], (SUBLANES, dh)).astype(BF16)
                s["qn"] = lax.dot_general(q_ref[rws, lanes[h]], n_rows, (((1,), (1,)), ((), ())),
                                          preferred_element_type=F32)[:, 0:1]
        for h, s in zip(heads, st):
            num, den = s["num"], s["den"]
            if not fresh:
                w_in = jnp.exp(s["inter"] - s["mr_q"])
                num = num + w_in * s["qc"]
                den = den + w_in * s["qn"]
            hout = num * (1.0 / jnp.maximum(jnp.abs(den), jnp.exp(-s["mr"])))
            h_ref[0, rws, lanes[h]] = hout.astype(h_ref.dtype)
        for h, s in zip(heads, st):
            g = jnp.sum(s["f_row"], axis=1, keepdims=True)
            lw = g - s["b_col"] + s["i_col"]
            s["m_new"] = jnp.maximum(g + s["m_prev"], jnp.max(lw, axis=0, keepdims=True))
            s["decay"] = jnp.exp(g + s["m_prev"] - s["m_new"])
            kw = k_ref[rws, lanes[h]].astype(F32) * jnp.exp(lw - s["m_new"])
            s["n_new"] = jnp.sum(kw, axis=0, keepdims=True)
            s["kw"] = kw.astype(BF16)
        for h, s in zip(heads, st):
            s["c_new"] = lax.dot_general(s["kw"], v_ref[rws, lanes[h]], (((0,), (0,)), ((), ())),
                                         preferred_element_type=F32)
        for h, s in zip(heads, st):
            if fresh:
                if emit_state:
                    cn_ref[0, 0, 0, h] = s["c_new"]
                    nn_ref[0, 0, 0, h:h + 1, :] = s["n_new"]
                    mn_ref[0, :, h:h + 1] = s["m_new"]
            else:
                c_new = s["decay"] * c_sc[h] + s["c_new"]
                c_sc[h] = c_new
                cb_sc[h] = c_new.astype(BF16)
                n_sc[h:h + 1, :] = s["decay"] * n_sc[h:h + 1, :] + s["n_new"]
                m_sc[:, h:h + 1] = s["m_new"]

    if emit_state and not fresh:
        @pl.when(c == nc - 1)
        def _():
            cn_ref[0, 0, 0] = c_sc[...]
            nn_ref[0, 0, 0] = n_sc[...]
            mn_ref[0] = m_sc[...]


def _mlstm_core_call(q, k, v, pre, state, *, row0, batch, seq, emit_state):
    di = q.shape[1]
    dh = di // M_HEADS
    chunk = min(MLSTM_CHUNK, seq)
    cps = min(MLSTM_CHUNKS_PER_STEP, seq // chunk)
    blk = cps * chunk
    nc = seq // blk
    chunk0 = row0 // blk
    has_state = state is not None

    def chunk_of(b, d, c):
        return b * nc + c + d * (nc - 1 - 2 * c)

    tok = pl.BlockSpec((blk, di), lambda b, d, c: (chunk0 + chunk_of(b, d, c), 0))
    in_specs = [tok, tok, tok,
                pl.BlockSpec((blk, LANES), lambda b, d, c: (chunk0 + chunk_of(b, d, c), 0))]
    args = [q, k, v, pre]
    state_c = pl.BlockSpec((1, 1, 1, M_HEADS, dh, dh), lambda b, d, c: (b, 0, d, 0, 0, 0))
    state_n = pl.BlockSpec((1, 1, 1, M_HEADS, dh), lambda b, d, c: (b, 0, d, 0, 0))
    state_m = pl.BlockSpec((1, 1, LANES), lambda b, d, c: (b * 2 + d, 0, 0))
    if has_state:
        in_specs += [state_c, state_n, state_m]
        args += list(state)
    out_specs = [pl.BlockSpec((1, blk, di), lambda b, d, c: (d, chunk_of(b, d, c), 0))]
    out_shape = [jax.ShapeDtypeStruct((2, batch * seq, di), BF16)]
    if emit_state:
        out_specs += [state_c, state_n, state_m]
        out_shape += [
            jax.ShapeDtypeStruct((batch, 1, 2, M_HEADS, dh, dh), F32),
            jax.ShapeDtypeStruct((batch, 1, 2, M_HEADS, dh), F32),
            jax.ShapeDtypeStruct((batch * 2, 1, LANES), F32),
        ]
    kern = functools.partial(_mlstm_core_kernel, nc=nc, dh=dh, chunk=chunk, cps=cps,
                             has_state=has_state, emit_state=emit_state)
    return pl.pallas_call(
        kern,
        grid=(batch, 2, nc),
        in_specs=in_specs,
        out_specs=out_specs,
        out_shape=out_shape,
        scratch_shapes=[pltpu.VMEM((M_HEADS, dh, dh), F32), pltpu.VMEM((M_HEADS, dh, dh), BF16),
                        pltpu.VMEM((M_HEADS, dh), F32), pltpu.VMEM((1, LANES), F32)],
        compiler_params=_params(("arbitrary",) * 3),
        name="mlstm_core_state" if emit_state else "mlstm_core",
    )(*args)


def _mdown_kernel(x_ref, mod_ref, o_ref, hpf_ref, hpb_ref, hsf_ref, hsb_ref, xc_ref, z_ref,
                  ng_ref, skip_ref, wd_ref, g_ref, b_ref, out_ref, *, prompt_tiles, dh, sub):
    i = pl.program_id(0)
    is_prompt = i < prompt_tiles
    n_sub = x_ref.shape[0] // sub
    st = [dict(rows=slice(t * sub, (t + 1) * sub)) for t in range(n_sub)]

    def gate(s):
        rows = s["rows"]
        hsum = (jnp.where(is_prompt, hpf_ref[0, rows, :], hsf_ref[0, rows, :]).astype(F32)
                + jnp.where(is_prompt, hpb_ref[0, rows, :], hsb_ref[0, rows, :]).astype(F32))
        parts = []
        for h in range(M_HEADS):
            hh = hsum[:, h * dh:(h + 1) * dh]
            mu = jnp.mean(hh, axis=-1, keepdims=True)
            hc = hh - mu
            var = jnp.mean(hc * hc, axis=-1, keepdims=True)
            parts.append(hc * lax.rsqrt(var + LN_EPS))
        hn = jnp.concatenate(parts, axis=-1) * ng_ref[...]
        mixed = ((o_ref[rows, :].astype(F32) * hn
                  + skip_ref[...] * xc_ref[rows, :].astype(F32))
                 * _silu(z_ref[rows, :].astype(F32)))
        s["mixed"] = mixed.astype(BF16)

    def project(s):
        s["y"] = _dot(s["mixed"], wd_ref[...])

    def finish(s):
        rows = s["rows"]
        out_ref[rows, :] = _post_norm(x_ref[rows, :], s["y"], mod_ref, 1, g_ref, b_ref, 1.0)

    stages = [gate, project, finish]
    for step in range(len(stages) + n_sub - 1):
        for t in range(n_sub):
            if 0 <= step - t < len(stages):
                stages[step - t](st[t])


def _mdown_call(x, mod, o, h_p, h_s, xc, z, norm_g, skip, w_down, g, b, *, group_rows):
    tm = MDOWN_SUBTILES * TM_MIX
    n, d = x.shape
    di = o.shape[1]
    prompt_tiles = h_p.shape[1] // tm
    row = pl.BlockSpec((tm, di), lambda i: (i, 0))
    xrow = pl.BlockSpec((tm, d), lambda i: (i, 0))

    def hspec(direction, first_group):
        if first_group:
            return pl.BlockSpec((1, tm, di),
                                lambda i: (direction, jnp.minimum(i, prompt_tiles - 1), 0))
        return pl.BlockSpec((1, tm, di),
                            lambda i: (direction, jnp.maximum(i - prompt_tiles, 0), 0))

    consts = [norm_g, skip, w_down, g, b]
    return pl.pallas_call(
        functools.partial(_mdown_kernel, prompt_tiles=prompt_tiles, dh=di // M_HEADS,
                          sub=TM_MIX),
        grid=(n // tm,),
        in_specs=[xrow, _mod_spec(d, group_rows // tm), row,
                  hspec(0, True), hspec(1, True), hspec(0, False), hspec(1, False), row, row]
        + [_resident(a.shape) for a in consts],
        out_specs=xrow,
        out_shape=jax.ShapeDtypeStruct((n, d), F32),
        compiler_params=_params(("arbitrary",)),
        name="mlstm_down",
    )(x, mod, o, h_p, h_p, h_s, h_s, xc, z, *consts)


def _conf_kernel(x_ref, mod_ref, w1_ref, b1_ref, wdw_ref, bdw_ref, lg_ref, lb_ref, w2_ref,
                 b2_ref, g_ref, b_ref, out_ref, sh_ref, conv_ref, *, tm, d, conv_w, seg_len):
    half = conv_w // 2
    gap = 2 * SUBLANES
    assert half < gap
    n_seg = tm // seg_len
    stride = seg_len + gap
    n_rows = gap + n_seg * stride
    conv_rows = 4 * SUBLANES

    groups = conv_rows // SUBLANES
    n_sub = x_ref.shape[0] // tm
    st = [dict(rows=slice(t * tm, (t + 1) * tm)) for t in range(n_sub)]

    def pointwise_in(t, s):
        s["x"] = x_ref[s["rows"], :]
        xm = _modulate(s["x"], mod_ref, 1).astype(BF16)
        s["ag"] = _dot(xm, w1_ref[...]) + b1_ref[...]

    def lay_out(t, s):
        glu = s["ag"][:, :d] * _sigmoid(s["ag"][:, d:])
        for g in range(n_seg + 1):
            sh_ref[t, 0, g * stride:g * stride + gap, :] = jnp.zeros((gap, d), F32)
        for g in range(n_seg):
            sh_ref[t, 0, gap + g * stride:gap + g * stride + seg_len, :] = (
                glu[g * seg_len:(g + 1) * seg_len, :])
        for r in range(1, SUBLANES):
            sh_ref[t, r, 0:n_rows - SUBLANES, :] = sh_ref[t, 0, r:r + n_rows - SUBLANES, :]

    def taps(t, s):
        for g in range(n_seg):
            for r0 in range(0, seg_len, conv_rows):
                acc = jnp.broadcast_to(bdw_ref[...], (groups, SUBLANES, d))
                for j in range(conv_w):
                    r = (j - half) % SUBLANES
                    y = gap + g * stride + r0 + (j - half) - r
                    tap = sh_ref[t, r, y:y + conv_rows, :].reshape(groups, SUBLANES, d)
                    acc = acc + tap * wdw_ref[j]
                c0 = g * seg_len + r0
                conv_ref[t, c0:c0 + conv_rows, :] = acc.reshape(conv_rows, d)

    def pointwise_out(t, s):
        hact = _silu(_layer_norm(conv_ref[t], lg_ref[...], lb_ref[...]))
        s["y"] = _dot(hact.astype(BF16), w2_ref[...]) + b2_ref[...]

    def finish(t, s):
        out_ref[s["rows"], :] = _post_norm(s["x"], s["y"], mod_ref, 1, g_ref, b_ref, 1.0)

    stages = [pointwise_in, lay_out, taps, pointwise_out, finish]
    for step in range(len(stages) + n_sub - 1):
        for t in range(n_sub):
            k = step - t
            if 0 <= k < len(stages):
                stages[k](t, st[t])


def _conf_call(x, mod, w1, b1, wdw, bdw, lg, lb, w2, b2, g, b, *, group_rows, row0, rows,
               seg_len):
    tm = TM_MIX
    blk = CONF_SUBTILES * tm
    _, d = x.shape
    assert tm % seg_len == 0 and seg_len % (4 * SUBLANES) == 0
    assert row0 % blk == 0 and rows % blk == 0 and group_rows % blk == 0
    tile0 = row0 // blk
    gap = 2 * SUBLANES
    n_rows = gap + (tm // seg_len) * (seg_len + gap)
    kern = functools.partial(_conf_kernel, tm=tm, d=d, conv_w=wdw.shape[0], seg_len=seg_len)
    consts = [w1, b1, wdw, bdw, lg, lb, w2, b2, g, b]
    return pl.pallas_call(
        kern,
        grid=(rows // blk,),
        in_specs=[pl.BlockSpec((blk, d), lambda i: (tile0 + i, 0)),
                  pl.BlockSpec((1, N_MOD, d),
                               lambda i: ((tile0 + i) // (group_rows // blk), 0, 0))]
        + [_resident(a.shape) for a in consts],
        out_specs=pl.BlockSpec((blk, d), lambda i: (i, 0)),
        out_shape=jax.ShapeDtypeStruct((rows, d), F32),
        scratch_shapes=[pltpu.VMEM((CONF_SUBTILES, SUBLANES, n_rows, d), F32),
                        pltpu.VMEM((CONF_SUBTILES, tm, d), F32)],
        compiler_params=_params(("arbitrary",)),
        name=f"conformer_conv_seg{seg_len}",
    )(x, mod, *consts)


def _row(a):
    return a.reshape(1, -1)


def kernel(x_prompt, x_sample, state_C, state_n, state_m, c, c_ctx, w_mod, b_mod, ln_g, ln_b, ffn_w_in, ffn_w_out, m_w_up, m_w_conv, m_b_conv, m_w_qk, m_w_v, m_w_gate, m_b_gate, m_w_o, m_b_o, m_norm_g, m_skip, m_w_down, cv_w_pw1, cv_b_pw1, cv_w_dw, cv_b_dw, cv_ln_g, cv_ln_b, cv_w_pw2, cv_b_pw2):
    bp, sp, d = x_prompt.shape
    bs, ss, _ = x_sample.shape
    n_p, n_s = bp * sp, bs * ss
    heads = M_HEADS
    group_rows = ss
    assert n_p == group_rows and sp & (sp - 1) == 0 and ss & (ss - 1) == 0
    assert 1 + bs <= COND_ROWS and GRID_W & (GRID_W - 1) == 0
    assert m_w_up.shape[0] == 1 and 4 * heads <= LANES

    cond = jnp.concatenate([c_ctx[None], c, jnp.zeros((COND_ROWS - 1 - bs, d), F32)], axis=0)
    mod_all = _mod_call(cond, w_mod, b_mod).reshape(DEPTH, COND_ROWS, N_MOD, d)

    xs = (x_prompt.reshape(n_p, d), x_sample.reshape(n_s, d))
    new_c = new_n = new_m = None
    ffn_w = (ffn_w_in[0, 0].astype(BF16), ffn_w_out[0, 0].astype(BF16))
    for i in range(DEPTH):
        mod = mod_all[i]
        ffn = functools.partial(_ffn_call, prompt_rows=n_p, group_rows=group_rows)
        x, *ffn_w = ffn(xs, mod, *ffn_w, _row(ln_g[i, 0]), _row(ln_b[i, 0]), s=0,
                        next_w=(ffn_w_in, ffn_w_out, i, 1))
        j = i // 2
        if i % 2 == 0:
            di = m_w_v.shape[1]
            cast_rows = 4 * BF16_SUBLANES
            xm, z, w_qk, w_v, w_o, w_down = _mup_call(
                x, mod, m_w_up[j].astype(BF16),
                [(w, (j,), cast_rows) for w in (m_w_qk, m_w_v, m_w_o, m_w_down)],
                group_rows=group_rows)
            n_gate = m_w_gate.shape[2]
            wg = jnp.pad(m_w_gate[j], ((0, 0), (0, LANES - n_gate))).astype(BF16)
            bg = jnp.pad(m_b_gate[j], (0, LANES - n_gate)).reshape(1, LANES)
            w_conv = jnp.broadcast_to(m_w_conv[j][:, None, :],
                                      (m_w_conv.shape[1], SUBLANES, di))
            xc, q, k, pre_qk = _mqk_call(
                xm, w_conv, _row(m_b_conv[j]), w_qk, wg[:2 * di],
                prompt_rows=n_p, prompt_seq=sp, sample_seq=ss)
            v, o, pre = _mvo_call(xm, pre_qk, w_v, w_o, _row(m_b_o[j]), wg[2 * di:], bg)
            h_p, new_c, new_n, m_p = _mlstm_core_call(
                q, k, v, pre, None, row0=0, batch=bp, seq=sp, emit_state=True)
            m0 = jnp.pad(state_m[:, j].reshape(bs * 2, 1, heads),
                         ((0, 0), (0, 0), (0, LANES - heads)))
            (h_s,) = _mlstm_core_call(
                q, k, v, pre, (state_C[:, j:j + 1], state_n[:, j:j + 1], m0),
                row0=n_p, batch=bs, seq=ss, emit_state=False)
            x = (_mdown_call(x, mod, o, h_p, h_s, xc, z, _row(m_norm_g[j]), _row(m_skip[j]),
                             w_down, _row(ln_g[i, 1]), _row(ln_b[i, 1]),
                             group_rows=group_rows),)
            new_m = m_p[:, 0, :heads].reshape(bp, 1, 2, heads)
        else:
            conf = functools.partial(
                _conf_call, x, mod, cv_w_pw1[j].astype(BF16), _row(cv_b_pw1[j]),
                jnp.broadcast_to(cv_w_dw[j][:, None, :], (cv_w_dw.shape[1], SUBLANES, d)),
                _row(cv_b_dw[j]), _row(cv_ln_g[j]), _row(cv_ln_b[j]), cv_w_pw2[j].astype(BF16),
                _row(cv_b_pw2[j]), _row(ln_g[i, 1]), _row(ln_b[i, 1]), group_rows=group_rows)
            x = (conf(row0=0, rows=n_p, seg_len=sp), conf(row0=n_p, rows=n_s, seg_len=GRID_W))
        last = i == DEPTH - 1
        outs = ffn(x, mod, *ffn_w, _row(ln_g[i, 2]), _row(ln_b[i, 2]), s=2, split_out=last,
                   next_w=None if last else (ffn_w_in, ffn_w_out, i + 1, 0))
        xs, ffn_w = (outs, None) if last else (outs[:1], outs[1:])
    y_p, y_s = xs
    return (y_p.reshape(bp, sp, d), y_s.reshape(bs, ss, d), new_c, new_n, new_m)
```
